```python
import math
import jax, jax.numpy as jnp
from jax import lax
import numpy as np

D_MODEL = 2048
BATCH = 4
SEQ = 4096
DEPTH = 4

CHUNK = 64
N_MIXERS = 2
RMS_EPS = 1e-6

GLA_HEADS = 4
GLA_QK = D_MODEL // 2
GLA_V = D_MODEL
GLA_DK = GLA_QK // GLA_HEADS
GLA_DV = GLA_V // GLA_HEADS
GLA_GATE_RANK = 16
GLA_GATE_TAU = 16.0
GLA_IN = 2 * GLA_QK + 2 * GLA_V + GLA_GATE_RANK

DSA_HEADS = 16
DSA_LATENT = 256
DSA_DV = D_MODEL // DSA_HEADS
IDX_HEADS = 16
IDX_DIM = 128
TOPK_MAX = 256
Q_BLOCK = 128
DSA_IN = DSA_HEADS * DSA_LATENT + DSA_LATENT + IDX_HEADS * IDX_DIM + IDX_DIM + IDX_HEADS

REL_BUCKETS = 32
REL_MAX_DIST = 128

D_FF = -(-8 * D_MODEL // (3 * 256)) * 256

N_GLA = (DEPTH + 1) // 2
N_DSA = DEPTH // 2

kernel_name = "hybrid_gla_dsa_streaming_trunk"


def rmsnorm(x, g):
    xf = x.astype(jnp.float32)
    y = xf * lax.rsqrt(jnp.mean(xf * xf, axis=-1, keepdims=True) + RMS_EPS)
    return (y * g.astype(jnp.float32)).astype(x.dtype)


def t5_bucket(rel):
    nb = REL_BUCKETS // 2
    max_exact = nb // 2
    ret = (rel > 0).astype(jnp.int32) * nb
    n = jnp.abs(rel)
    large = max_exact + (jnp.log(jnp.maximum(n, 1).astype(jnp.float32) / max_exact)
                         / math.log(REL_MAX_DIST / max_exact) * (nb - max_exact)).astype(jnp.int32)
    large = jnp.minimum(large, nb - 1)
    return ret + jnp.where(n < max_exact, n, large)


def gla_mixer(h, w_in, w_a2, b_a, g_norm, w_out):
    B, T, _ = h.shape
    proj = h @ w_in
    q, k, v, g, a = jnp.split(proj, [GLA_QK, 2 * GLA_QK, 2 * GLA_QK + GLA_V, 2 * GLA_QK + 2 * GLA_V], axis=-1)
    q = q.reshape(B, T, GLA_HEADS, GLA_DK) * (GLA_DK ** -0.5)
    k = k.reshape(B, T, GLA_HEADS, GLA_DK)
    v = v.reshape(B, T, GLA_HEADS, GLA_DV)
    log_alpha = jax.nn.log_sigmoid((a @ w_a2 + b_a).astype(jnp.float32)) / GLA_GATE_TAU
    log_alpha = log_alpha.reshape(B, T, GLA_HEADS, GLA_DK)
    nc = T // CHUNK

    def to_chunks(z):
        return jnp.moveaxis(z.reshape(B, nc, CHUNK, *z.shape[2:]), 1, 0)

    def step(state, xs):
        qc, kc, vc, lac = xs
        lcum = jnp.cumsum(lac, axis=1)
        ltot = lcum[:, -1]
        kd = kc.astype(jnp.float32) * jnp.exp(ltot[:, None] - lcum)
        state = state * jnp.exp(ltot)[..., None] + jnp.einsum('bchk,bchv->bhkv', kd, vc.astype(jnp.float32))
        oc = jnp.einsum('bchk,bhkv->bchv', qc.astype(jnp.float32), state)
        return state, oc

    s0 = jnp.zeros((B, GLA_HEADS, GLA_DK, GLA_DV), jnp.float32)
    _, o = lax.scan(step, s0, (to_chunks(q), to_chunks(k), to_chunks(v), to_chunks(log_alpha)))
    o = jnp.moveaxis(o, 0, 1).reshape(B, T, GLA_HEADS, GLA_DV)
    o = rmsnorm(o, g_norm).reshape(B, T, GLA_V).astype(h.dtype)
    o = o * jax.nn.silu(g)
    return o @ w_out


def dsa_mixer(h, w_in, kv_norm, kidx_norm, w_uv, w_out, rel_bias):
    B, T, _ = h.shape
    proj = h @ w_in
    s1 = DSA_HEADS * DSA_LATENT
    s2 = s1 + DSA_LATENT
    s3 = s2 + IDX_HEADS * IDX_DIM
    s4 = s3 + IDX_DIM
    q, c, qi, ki, wi = jnp.split(proj, [s1, s2, s3, s4], axis=-1)
    q = q.reshape(B, T, DSA_HEADS, DSA_LATENT)
    c = rmsnorm(c, kv_norm)
    qi = qi.reshape(B, T, IDX_HEADS, IDX_DIM)
    ki = rmsnorm(ki, kidx_norm).astype(jnp.float32)
    wi = wi * (IDX_HEADS ** -0.5)
    k_sel = min(TOPK_MAX, T // 4)
    pos = jnp.arange(T, dtype=jnp.int32)
    key_chunk = pos // CHUNK
    nb = T // Q_BLOCK
    c32 = c.astype(jnp.float32)
    w_uv32 = w_uv.astype(jnp.float32)

    def blk(z):
        return jnp.moveaxis(z.reshape(B, nb, Q_BLOCK, *z.shape[2:]), 1, 0)

    def attend(xs):
        qb, qib, wib, tq = xs
        sc = jax.nn.relu(jnp.einsum('bqhd,bsd->bqhs', qib.astype(jnp.float32), ki) * (IDX_DIM ** -0.5))
        score = jnp.einsum('bqhs,bqh->bqs', sc, wib.astype(jnp.float32))
        q_chunk = tq // CHUNK
        adm = key_chunk[None, :] <= q_chunk[:, None]
        score = jnp.where(adm[None], score, -jnp.inf)
        _, idx = lax.top_k(score, k_sel)
        c_sel = jax.vmap(lambda cb, ib: cb[ib])(c32, idx)
        logits = jnp.einsum('bqhd,bqkd->bqhk', qb.astype(jnp.float32), c_sel) * (DSA_LATENT ** -0.5)
        bias = rel_bias.astype(jnp.float32)[t5_bucket(idx - tq[None, :, None])]
        logits = logits + jnp.swapaxes(bias, -1, -2)
        valid = (idx // CHUNK) <= q_chunk[None, :, None]
        logits = jnp.where(valid[:, :, None, :], logits, -jnp.inf)
        p = jax.nn.softmax(logits, axis=-1)
        ob = jnp.einsum('bqhk,bqkd->bqhd', p, c_sel)
        return jnp.einsum('bqhd,hde->bqhe', ob, w_uv32)

    o = lax.map(attend, (blk(q), blk(qi), blk(wi), pos.reshape(nb, Q_BLOCK)))
    o = jnp.moveaxis(o, 0, 1).reshape(B, T, DSA_HEADS * DSA_DV).astype(h.dtype)
    return o @ w_out


def swiglu(h, w1, w3, w2):
    return (jax.nn.silu(h @ w1) * (h @ w3)) @ w2


def setup_inputs(seed: int = 0) -> dict:
    key = jax.random.key(seed)
    ks = jax.random.split(key, 24)
    f32 = jnp.float32

    def nrm(k, shape, scale):
        return jax.random.normal(k, shape, f32) * scale

    out_scale = (2.0 * DEPTH) ** -0.5
    return {
        "x": nrm(ks[0], (BATCH, SEQ, D_MODEL), 1.0),
        "norm_mix": 1.0 + nrm(ks[1], (DEPTH, D_MODEL), 0.05),
        "norm_ffn": 1.0 + nrm(ks[2], (DEPTH, D_MODEL), 0.05),
        "norm_final": 1.0 + nrm(ks[3], (D_MODEL,), 0.05),
        "gla_w_in": nrm(ks[4], (N_GLA, D_MODEL, GLA_IN), D_MODEL ** -0.5),
        "gla_w_a2": nrm(ks[5], (N_GLA, GLA_GATE_RANK, GLA_QK), GLA_GATE_RANK ** -0.5),
        "gla_b_a": nrm(ks[6], (N_GLA, GLA_QK), 0.1),
        "gla_g_norm": 1.0 + nrm(ks[7], (N_GLA, GLA_DV), 0.05),
        "gla_w_out": nrm(ks[8], (N_GLA, GLA_V, D_MODEL), GLA_V ** -0.5 * out_scale),
        "dsa_w_in": nrm(ks[9], (N_DSA, D_MODEL, DSA_IN), D_MODEL ** -0.5),
        "dsa_kv_norm": 1.0 + nrm(ks[10], (N_DSA, DSA_LATENT), 0.05),
        "dsa_kidx_norm": 1.0 + nrm(ks[11], (N_DSA, IDX_DIM), 0.05),
        "dsa_w_uv": nrm(ks[12], (N_DSA, DSA_HEADS, DSA_LATENT, DSA_DV), DSA_LATENT ** -0.5),
        "dsa_w_out": nrm(ks[13], (N_DSA, DSA_HEADS * DSA_DV, D_MODEL), (DSA_HEADS * DSA_DV) ** -0.5 * out_scale),
        "rel_bias": nrm(ks[14], (REL_BUCKETS, DSA_HEADS), 0.5),
        "ffn_w1": nrm(ks[15], (DEPTH, D_MODEL, D_FF), D_MODEL ** -0.5),
        "ffn_w3": nrm(ks[16], (DEPTH, D_MODEL, D_FF), D_MODEL ** -0.5),
        "ffn_w2": nrm(ks[17], (DEPTH, D_FF, D_MODEL), D_FF ** -0.5 * out_scale),
    }


def reference(x, norm_mix, norm_ffn, norm_final,
              gla_w_in, gla_w_a2, gla_b_a, gla_g_norm, gla_w_out,
              dsa_w_in, dsa_kv_norm, dsa_kidx_norm, dsa_w_uv, dsa_w_out,
              rel_bias, ffn_w1, ffn_w3, ffn_w2):
    for i in range(DEPTH):
        h = rmsnorm(x, norm_mix[i])
        j = i // N_MIXERS
        if i % N_MIXERS == 0:
            y = gla_mixer(h, gla_w_in[j], gla_w_a2[j], gla_b_a[j], gla_g_norm[j], gla_w_out[j])
        else:
            y = dsa_mixer(h, dsa_w_in[j], dsa_kv_norm[j], dsa_kidx_norm[j], dsa_w_uv[j], dsa_w_out[j], rel_bias)
        x = x + y
        x = x + swiglu(rmsnorm(x, norm_ffn[i]), ffn_w1[i], ffn_w3[i], ffn_w2[i])
    return rmsnorm(x, norm_final)
```

```python
import functools
import math

import jax
import jax.numpy as jnp
from jax import lax
from jax.experimental import pallas as pl
from jax.experimental.pallas import tpu as pltpu

F32 = jnp.float32
BF16 = jnp.bfloat16

RMS_EPS = 1e-6
CHUNK = 64

GLA_HEADS = 4
GLA_GATE_RANK = 16
GLA_GATE_TAU = 16.0

DSA_HEADS = 16
DSA_LATENT = 256
IDX_HEADS = 16
IDX_DIM = 128
TOPK_MAX = 256
Q_BLOCK = 128

REL_BUCKETS = 32
REL_MAX_DIST = 128

LANE = 128
FAR_W = 512
NEAR_W = 2 * Q_BLOCK
MASK_NEG = -1e30
MAX_BISECT = 200
VMEM_LIMIT = 56 * 1024 * 1024

_NT = (((1,), (1,)), ((), ()))


def _cparams(sem):
    return pltpu.CompilerParams(dimension_semantics=sem, vmem_limit_bytes=VMEM_LIMIT)


def _rmsnorm_rows(dst_ref, x_ref, g_ref, rows):
    def body(r, c):
        sl = pl.ds(pl.multiple_of(r * rows, rows), rows)
        x = x_ref[sl, :]
        ms = jnp.mean(x * x, axis=-1, keepdims=True)
        dst_ref[sl, :] = (x * lax.rsqrt(ms + RMS_EPS) * g_ref[...]).astype(dst_ref.dtype)
        return c
    lax.fori_loop(0, x_ref.shape[0] // rows, body, 0)


def _norm_mm_kernel(x_ref, g_ref, w_ref, o_ref, h_ref):
    @pl.when(pl.program_id(1) == 0)
    def _():
        _rmsnorm_rows(h_ref, x_ref, g_ref, 128)
    o_ref[...] = jnp.dot(h_ref[...], w_ref[...], preferred_element_type=F32).astype(o_ref.dtype)


def _norm_matmul(x, g, w, tm, tn):
    n, d = x.shape
    m = w.shape[1]
    return pl.pallas_call(
        _norm_mm_kernel,
        grid=(n // tm, m // tn),
        in_specs=[pl.BlockSpec((tm, d), lambda i, j: (i, 0)),
                  pl.BlockSpec((1, d), lambda i, j: (0, 0)),
                  pl.BlockSpec((d, tn), lambda i, j: (0, j))],
        out_specs=pl.BlockSpec((tm, tn), lambda i, j: (i, j)),
        out_shape=jax.ShapeDtypeStruct((n, m), F32),
        scratch_shapes=[pltpu.VMEM((tm, d), BF16)],
        compiler_params=_cparams(("parallel", "arbitrary")),
        name="norm_matmul",
    )(x, g.reshape(1, d), w)


def _norm_glu_kernel(x_ref, g_ref, w1_ref, w3_ref, o_ref, h_ref):
    @pl.when(pl.program_id(1) == 0)
    def _():
        _rmsnorm_rows(h_ref, x_ref, g_ref, 128)
    h = h_ref[...]
    a = jnp.dot(h, w1_ref[...], preferred_element_type=F32)
    b = jnp.dot(h, w3_ref[...], preferred_element_type=F32)
    o_ref[...] = (a * (1.0 / (1.0 + jnp.exp(-a))) * b).astype(o_ref.dtype)


def _norm_glu(x, g, w1, w3, tm, tn):
    n, d = x.shape
    m = w1.shape[1]
    return pl.pallas_call(
        _norm_glu_kernel,
        grid=(n // tm, m // tn),
        in_specs=[pl.BlockSpec((tm, d), lambda i, j: (i, 0)),
                  pl.BlockSpec((1, d), lambda i, j: (0, 0)),
                  pl.BlockSpec((d, tn), lambda i, j: (0, j)),
                  pl.BlockSpec((d, tn), lambda i, j: (0, j))],
        out_specs=pl.BlockSpec((tm, tn), lambda i, j: (i, j)),
        out_shape=jax.ShapeDtypeStruct((n, m), BF16),
        scratch_shapes=[pltpu.VMEM((tm, d), BF16)],
        compiler_params=_cparams(("parallel", "arbitrary")),
        name="norm_glu",
    )(x, g.reshape(1, d), w1, w3)


def _mm_res_kernel(a_ref, w_ref, r_ref, o_ref):
    o_ref[...] = r_ref[...] + jnp.dot(a_ref[...], w_ref[...], preferred_element_type=F32)


def _matmul_residual(a, w, res, tm, tn):
    n, k = a.shape
    m = w.shape[1]
    return pl.pallas_call(
        _mm_res_kernel,
        grid=(n // tm, m // tn),
        in_specs=[pl.BlockSpec((tm, k), lambda i, j: (i, 0)),
                  pl.BlockSpec((k, tn), lambda i, j: (0, j)),
                  pl.BlockSpec((tm, tn), lambda i, j: (i, j))],
        out_specs=pl.BlockSpec((tm, tn), lambda i, j: (i, j)),
        out_shape=jax.ShapeDtypeStruct((n, m), F32),
        compiler_params=_cparams(("parallel", "arbitrary")),
        name="matmul_residual",
    )(a, w, res)


def _final_norm_kernel(x_ref, g_ref, o_ref):
    _rmsnorm_rows(o_ref, x_ref, g_ref, 128)


def _final_norm(x, g, tm):
    n, d = x.shape
    return pl.pallas_call(
        _final_norm_kernel,
        grid=(n // tm,),
        in_specs=[pl.BlockSpec((tm, d), lambda i: (i, 0)),
                  pl.BlockSpec((1, d), lambda i: (0, 0))],
        out_specs=pl.BlockSpec((tm, d), lambda i: (i, 0)),
        out_shape=jax.ShapeDtypeStruct((n, d), F32),
        compiler_params=_cparams(("parallel",)),
        name="final_norm",
    )(x, g.reshape(1, d))


def _split3_bf16(x):
    hi = x.astype(BF16)
    r1 = x - hi.astype(F32)
    mid = r1.astype(BF16)
    lo = (r1 - mid.astype(F32)).astype(BF16)
    return hi, mid, lo


def _gla_kernel(q_ref, k_ref, v_ref, g_ref, a_ref, wa2_ref, ba_ref, gn_ref, o_ref, st_ref,
                *, heads, dk, dv):
    tb = q_ref.shape[0]

    @pl.when(pl.program_id(1) == 0)
    def _():
        st_ref[...] = jnp.zeros_like(st_ref)

    z = jnp.dot(a_ref[...].astype(BF16), wa2_ref[...], preferred_element_type=F32) + ba_ref[...]
    la = (jnp.minimum(z, 0.0) - jnp.log1p(jnp.exp(-jnp.abs(z)))) * (1.0 / GLA_GATE_TAU)

    ri = lax.broadcasted_iota(jnp.int32, (CHUNK, CHUNK), 0)
    ci = lax.broadcasted_iota(jnp.int32, (CHUNK, CHUNK), 1)
    tri = jnp.where(ri >= ci, 1.0, 0.0).astype(BF16)
    q_scale = dk ** -0.5

    for c in range(tb // CHUNK):
        rs = slice(c * CHUNK, (c + 1) * CHUNK)
        hi, mid, lo = _split3_bf16(la[rs, :])
        lcum = (jnp.dot(tri, hi, preferred_element_type=F32)
                + jnp.dot(tri, mid, preferred_element_type=F32)
                + jnp.dot(tri, lo, preferred_element_type=F32))
        ltot = lcum[CHUNK - 1:CHUNK, :]
        kd = (k_ref[rs, :] * jnp.exp(ltot - lcum)).astype(BF16)
        dec = jnp.exp(ltot)
        qc = (q_ref[rs, :] * q_scale).astype(BF16)
        for h in range(heads):
            ks = slice(h * dk, (h + 1) * dk)
            vs = slice(h * dv, (h + 1) * dv)
            vt = v_ref[rs, vs].T.astype(BF16)
            s_new = st_ref[h] * dec[:, ks] + jnp.dot(vt, kd[:, ks], preferred_element_type=F32)
            st_ref[h] = s_new
            oh = lax.dot_general(qc[:, ks], s_new.astype(BF16), _NT,
                                 preferred_element_type=F32)
            ms = jnp.mean(oh * oh, axis=-1, keepdims=True)
            y = oh * lax.rsqrt(ms + RMS_EPS) * gn_ref[...]
            gate = g_ref[rs, vs]
            y = y * (gate * (1.0 / (1.0 + jnp.exp(-gate))))
            o_ref[rs, vs] = y.astype(o_ref.dtype)


def _gla_scan(proj, wa2p, ba, gn, batch, seq, d_qk, d_v, tb):
    n = proj.shape[0]
    nt = seq // tb
    heads = GLA_HEADS
    dk, dv = d_qk // heads, d_v // heads
    row = lambda b, t: b * nt + t
    kern = functools.partial(_gla_kernel, heads=heads, dk=dk, dv=dv)
    return pl.pallas_call(
        kern,
        grid=(batch, nt),
        in_specs=[pl.BlockSpec((tb, d_qk), lambda b, t: (row(b, t), 0)),
                  pl.BlockSpec((tb, d_qk), lambda b, t: (row(b, t), 1)),
                  pl.BlockSpec((tb, d_v), lambda b, t: (row(b, t), (2 * d_qk) // d_v)),
                  pl.BlockSpec((tb, d_v), lambda b, t: (row(b, t), (2 * d_qk + d_v) // d_v)),
                  pl.BlockSpec((tb, LANE), lambda b, t: (row(b, t), (2 * d_qk + 2 * d_v) // LANE)),
                  pl.BlockSpec((LANE, d_qk), lambda b, t: (0, 0)),
                  pl.BlockSpec((1, d_qk), lambda b, t: (0, 0)),
                  pl.BlockSpec((1, dv), lambda b, t: (0, 0))],
        out_specs=pl.BlockSpec((tb, d_v), lambda b, t: (row(b, t), 0)),
        out_shape=jax.ShapeDtypeStruct((n, d_v), BF16),
        scratch_shapes=[pltpu.VMEM((heads, dv, dk), F32)],
        compiler_params=_cparams(("arbitrary", "arbitrary")),
        name="gla_scan",
    )(proj, proj, proj, proj, proj, wa2p, ba.reshape(1, d_qk), gn.reshape(1, dv))


def _t5_bucket(rel):
    nb = REL_BUCKETS // 2
    max_exact = nb // 2
    ret = (rel > 0).astype(jnp.int32) * nb
    n = jnp.abs(rel)
    large = max_exact + (jnp.log(jnp.maximum(n, 1).astype(jnp.float32) / max_exact)
                         / math.log(REL_MAX_DIST / max_exact) * (nb - max_exact)).astype(jnp.int32)
    large = jnp.minimum(large, nb - 1)
    return ret + jnp.where(n < max_exact, n, large)


def _dsa_kernel(q_ref, qi_ref, c_ref, ki_ref, wi_ref, kvn_ref, kin_ref, wuv_ref, bkt_ref, rb_ref,
                o_ref,
                cpad_ref, kipad_ref, bias_ref, qs_ref, qis_ref, scf_ref, scn_ref,
                m_ref, l_ref, al_ref, acc_ref, p_ref, *, topk):
    b = pl.program_id(0)
    j = pl.program_id(1)
    H, QB, LAT = DSA_HEADS, Q_BLOCK, DSA_LATENT
    far_bucket = REL_BUCKETS // 2 - 1

    @pl.when((b == 0) & (j == 0))
    def _init():
        cpad_ref[...] = jnp.zeros_like(cpad_ref)
        kipad_ref[...] = jnp.zeros_like(kipad_ref)
        bk = bkt_ref[...]

        def head_bias(h, c):
            far = rb_ref[far_bucket, h]
            t = jnp.zeros(bk.shape, F32)
            for bb in range(REL_BUCKETS):
                t = jnp.where(bk == bb, rb_ref[bb, h] - far, t)
            bias_ref[h] = t
            return c
        lax.fori_loop(0, H, head_bias, 0)

    r_own = pl.multiple_of((j + 1) * QB, QB)
    c = c_ref[...]
    cn = c * lax.rsqrt(jnp.mean(c * c, axis=-1, keepdims=True) + RMS_EPS) * kvn_ref[...]
    cpad_ref[pl.ds(r_own, QB), :] = cn.astype(BF16)
    ki = ki_ref[...]
    kin = ki * lax.rsqrt(jnp.mean(ki * ki, axis=-1, keepdims=True) + RMS_EPS) * kin_ref[...]
    kipad_ref[pl.ds(r_own, QB), :] = kin.astype(BF16)

    for h in range(H):
        qs_ref[h * QB:(h + 1) * QB, :] = (q_ref[:, h * LAT:(h + 1) * LAT] * (LAT ** -0.5)).astype(BF16)
        qis_ref[h * QB:(h + 1) * QB, :] = qi_ref[:, h * IDX_DIM:(h + 1) * IDX_DIM].astype(BF16)
    wi = wi_ref[...] * (IDX_HEADS ** -0.5 * IDX_DIM ** -0.5)
    wcols = [wi[:, h:h + 1] for h in range(IDX_HEADS)]

    nf = lax.shift_right_logical(j + 2, 2)
    tq = lax.broadcasted_iota(jnp.int32, (QB, 1), 0)
    tq_chunk = lax.shift_right_logical(tq, int(math.log2(CHUNK)))
    colf = lax.broadcasted_iota(jnp.int32, (QB, FAR_W), 1)
    coln = lax.broadcasted_iota(jnp.int32, (QB, NEAR_W), 1)
    adm_n = (coln < CHUNK * (QB // CHUNK + 1 + tq_chunk)) & ((j >= 1) | (coln >= QB))

    def score_of(kib):
        sc = lax.dot_general(qis_ref[...], kib, _NT, preferred_element_type=F32)
        s = jnp.maximum(sc[0:QB, :], 0.0) * wcols[0]
        for h in range(1, IDX_HEADS):
            s = s + jnp.maximum(sc[h * QB:(h + 1) * QB, :], 0.0) * wcols[h]
        return s

    def far_rows(mb):
        return pl.ds(pl.multiple_of(QB + mb * FAR_W, QB), FAR_W)

    def far_score(mb, carry):
        mn, mx = carry
        s = score_of(kipad_ref[far_rows(mb), :])
        adm = colf < (j - 1) * QB - mb * FAR_W
        scf_ref[mb] = jnp.where(adm, s, -jnp.inf)
        mx = jnp.maximum(mx, jnp.max(jnp.where(adm, s, -jnp.inf), axis=1, keepdims=True))
        mn = jnp.minimum(mn, jnp.min(jnp.where(adm, s, jnp.inf), axis=1, keepdims=True))
        return mn, mx

    mn, mx = lax.fori_loop(0, nf, far_score,
                           (jnp.full((QB, 1), jnp.inf, F32), jnp.full((QB, 1), -jnp.inf, F32)))
    r_near = pl.multiple_of(j * QB, QB)
    s = score_of(kipad_ref[pl.ds(r_near, NEAR_W), :])
    scn_ref[...] = jnp.where(adm_n, s, -jnp.inf)
    mx = jnp.maximum(mx, jnp.max(jnp.where(adm_n, s, -jnp.inf), axis=1, keepdims=True))
    mn = jnp.minimum(mn, jnp.min(jnp.where(adm_n, s, jnp.inf), axis=1, keepdims=True))

    kf = float(topk)

    def count_ge(mid):
        def fb(mb, acc):
            g = jnp.where(scf_ref[mb] >= mid, 1.0, 0.0)
            for u in range(FAR_W // LANE):
                acc = acc + g[:, u * LANE:(u + 1) * LANE]
            return acc
        acc = lax.fori_loop(0, nf, fb, jnp.zeros((QB, LANE), F32))
        g = jnp.where(scn_ref[...] >= mid, 1.0, 0.0)
        for u in range(NEAR_W // LANE):
            acc = acc + g[:, u * LANE:(u + 1) * LANE]
        return jnp.sum(acc, axis=1, keepdims=True)

    def bis_cond(carry):
        it, _, _, _, pending = carry
        return (pending > 0.0) & (it < MAX_BISECT)

    def bis_body(carry):
        it, lo, hi, cnt, _ = carry
        mid = 0.5 * lo + 0.5 * hi
        cm = count_ge(mid)
        take = cm >= kf
        stuck = (mid <= lo) | (mid >= hi)
        lo2 = jnp.where(take, mid, lo)
        hi2 = jnp.where(take, hi, mid)
        cnt2 = jnp.where(take, cm, cnt)
        done = (cnt2 <= kf) | stuck
        return it + 1, lo2, hi2, cnt2, jnp.max(jnp.where(done, 0.0, 1.0))

    n_adm = ((2 * j + tq_chunk + 1) * CHUNK).astype(F32)
    pending0 = jnp.max(jnp.where(n_adm <= kf, 0.0, 1.0))
    _, lo, _, _, _ = lax.while_loop(bis_cond, bis_body, (jnp.int32(0), mn, mx, n_adm, pending0))

    m_ref[...] = jnp.full(m_ref.shape, MASK_NEG, F32)
    l_ref[...] = jnp.zeros_like(l_ref)
    acc_ref[...] = jnp.zeros_like(acc_ref)

    def attend(cb, madd, near):
        w = cb.shape[0]
        lg = lax.dot_general(qs_ref[...], cb, _NT, preferred_element_type=F32)
        for h in range(H):
            hs = slice(h * QB, (h + 1) * QB)
            sh = lg[hs, :] + madd
            if near:
                sh = sh + bias_ref[h]
            m_old = m_ref[hs, :]
            m_new = jnp.maximum(m_old, jnp.max(sh, axis=1, keepdims=True))
            alpha = jnp.exp(m_old - m_new)
            p = jnp.exp(sh - m_new)
            l_ref[hs, :] = alpha * l_ref[hs, :] + jnp.sum(p, axis=1, keepdims=True)
            m_ref[hs, :] = m_new
            al_ref[hs, :] = alpha
            p_ref[hs, 0:w] = p.astype(BF16)
        pv = jnp.dot(p_ref[:, 0:w], cb, preferred_element_type=F32)
        acc_ref[...] = acc_ref[...] * al_ref[...] + pv

    def far_attend(mb, carry):
        madd = jnp.where(scf_ref[mb] >= lo, 0.0, MASK_NEG)
        attend(cpad_ref[far_rows(mb), :], madd, False)
        return carry

    lax.fori_loop(0, nf, far_attend, 0)
    madd = jnp.where(scn_ref[...] >= lo, 0.0, MASK_NEG)
    attend(cpad_ref[pl.ds(r_near, NEAR_W), :], madd, True)

    dvh = wuv_ref.shape[2]
    for h in range(H):
        hs = slice(h * QB, (h + 1) * QB)
        oh = (acc_ref[hs, :] * (1.0 / l_ref[hs, :])).astype(BF16)
        o_ref[:, h * dvh:(h + 1) * dvh] = jnp.dot(
            oh, wuv_ref[h], preferred_element_type=F32).astype(o_ref.dtype)


def _dsa_attention(proj, kvn, kin, wuv, bkt, rel_bias, batch, seq, d_out):
    n = proj.shape[0]
    nq = seq // Q_BLOCK
    H, QB, LAT = DSA_HEADS, Q_BLOCK, DSA_LATENT
    topk = min(TOPK_MAX, seq // 4)
    row = lambda b, j: b * nq + j
    o_qi = H * LAT
    o_c = o_qi + IDX_HEADS * IDX_DIM
    o_ki = o_c + LAT
    o_wi = o_ki + IDX_DIM
    n_far = max(1, (nq + 1) // 4)
    const = lambda b, j: (0, 0)
    return pl.pallas_call(
        functools.partial(_dsa_kernel, topk=topk),
        grid=(batch, nq),
        in_specs=[pl.BlockSpec((QB, H * LAT), lambda b, j: (row(b, j), 0)),
                  pl.BlockSpec((QB, IDX_HEADS * IDX_DIM), lambda b, j: (row(b, j), o_qi // (IDX_HEADS * IDX_DIM))),
                  pl.BlockSpec((QB, LAT), lambda b, j: (row(b, j), o_c // LAT)),
                  pl.BlockSpec((QB, IDX_DIM), lambda b, j: (row(b, j), o_ki // IDX_DIM)),
                  pl.BlockSpec((QB, LANE), lambda b, j: (row(b, j), o_wi // LANE)),
                  pl.BlockSpec((1, LAT), const),
                  pl.BlockSpec((1, IDX_DIM), const),
                  pl.BlockSpec(wuv.shape, lambda b, j: (0, 0, 0)),
                  pl.BlockSpec((QB, NEAR_W), const),
                  pl.BlockSpec(memory_space=pltpu.SMEM)],
        out_specs=pl.BlockSpec((QB, d_out), lambda b, j: (row(b, j), 0)),
        out_shape=jax.ShapeDtypeStruct((n, d_out), BF16),
        scratch_shapes=[pltpu.VMEM((seq + QB, LAT), BF16),
                        pltpu.VMEM((seq + QB, IDX_DIM), BF16),
                        pltpu.VMEM((H, QB, NEAR_W), F32),
                        pltpu.VMEM((H * QB, LAT), BF16),
                        pltpu.VMEM((H * QB, IDX_DIM), BF16),
                        pltpu.VMEM((n_far, QB, FAR_W), F32),
                        pltpu.VMEM((QB, NEAR_W), F32),
                        pltpu.VMEM((H * QB, 1), F32),
                        pltpu.VMEM((H * QB, 1), F32),
                        pltpu.VMEM((H * QB, 1), F32),
                        pltpu.VMEM((H * QB, LAT), F32),
                        pltpu.VMEM((H * QB, FAR_W), BF16)],
        compiler_params=_cparams(("arbitrary", "arbitrary")),
        name="dsa_attention",
    )(proj, proj, proj, proj, proj, kvn.reshape(1, LAT), kin.reshape(1, IDX_DIM), wuv, bkt, rel_bias)


def _pad_cols(w, mult):
    pad = (-w.shape[1]) % mult
    return jnp.pad(w, ((0, 0), (0, pad))) if pad else w


def kernel(x, norm_mix, norm_ffn, norm_final, gla_w_in, gla_w_a2, gla_b_a, gla_g_norm, gla_w_out,
           dsa_w_in, dsa_kv_norm, dsa_kidx_norm, dsa_w_uv, dsa_w_out, rel_bias, ffn_w1, ffn_w3, ffn_w2):
    batch, seq, d = x.shape
    depth = norm_mix.shape[0]
    n = batch * seq
    d_qk = gla_w_a2.shape[2]
    d_v = gla_w_out.shape[1]
    tn = 512
    tm = min(1024, n)

    tq = jnp.arange(Q_BLOCK, dtype=jnp.int32)[:, None]
    wk = jnp.arange(NEAR_W, dtype=jnp.int32)[None, :]
    bkt = _t5_bucket(wk - Q_BLOCK - tq)

    s1 = DSA_HEADS * DSA_LATENT
    s2 = s1 + DSA_LATENT
    s3 = s2 + IDX_HEADS * IDX_DIM
    s4 = s3 + IDX_DIM

    xf = x.reshape(n, d)
    for i in range(depth):
        jm = i // 2
        if i % 2 == 0:
            w_in = _pad_cols(gla_w_in[jm].astype(BF16), tn)
            wa2p = jnp.pad(gla_w_a2[jm].astype(BF16), ((0, LANE - GLA_GATE_RANK), (0, 0)))
            proj = _norm_matmul(xf, norm_mix[i], w_in, tm, tn)
            u = _gla_scan(proj, wa2p, gla_b_a[jm], gla_g_norm[jm], batch, seq, d_qk, d_v, min(256, seq))
            xf = _matmul_residual(u, gla_w_out[jm].astype(BF16), xf, tm, tn)
        else:
            w = dsa_w_in[jm]
            w_in = jnp.concatenate([w[:, :s1], w[:, s2:s3], w[:, s1:s2], w[:, s3:s4], w[:, s4:]], axis=1)
            w_in = _pad_cols(w_in.astype(BF16), tn)
            proj = _norm_matmul(xf, norm_mix[i], w_in, tm, tn)
            u = _dsa_attention(proj, dsa_kv_norm[jm], dsa_kidx_norm[jm], dsa_w_uv[jm].astype(BF16),
                               bkt, rel_bias, batch, seq, dsa_w_out.shape[1])
            xf = _matmul_residual(u, dsa_w_out[jm].astype(BF16), xf, tm, tn)
        act = _norm_glu(xf, norm_ffn[i], ffn_w1[i].astype(BF16), ffn_w3[i].astype(BF16), tm, tn)
        xf = _matmul_residual(act, ffn_w2[i].astype(BF16), xf, min(512, n), tn)
    return _final_norm(xf, norm_final, min(512, n)).reshape(batch, seq, d)
```

```python
import functools
import math

import jax
import jax.numpy as jnp
from jax import lax
from jax.experimental import pallas as pl
from jax.experimental.pallas import tpu as pltpu

F32 = jnp.float32
BF16 = jnp.bfloat16

RMS_EPS = 1e-6
CHUNK = 64

GLA_HEADS = 4
GLA_GATE_RANK = 16
GLA_GATE_TAU = 16.0

DSA_HEADS = 16
DSA_LATENT = 256
IDX_HEADS = 16
IDX_DIM = 128
TOPK_MAX = 256
Q_BLOCK = 128

REL_BUCKETS = 32
REL_MAX_DIST = 128

LANE = 128
SUBLANE = 8
FAR_W = 512
NEAR_W = 2 * Q_BLOCK
MASK_NEG = -1e30
BISECT_UNROLL = 4
MAX_BISECT = 200
VMEM_LIMIT = 56 * 1024 * 1024

_NT = (((1,), (1,)), ((), ()))


def _cparams(sem):
    return pltpu.CompilerParams(dimension_semantics=sem, vmem_limit_bytes=VMEM_LIMIT)


def _rmsnorm_rows(dst_ref, x_ref, g_ref, rows):
    def body(r, c):
        sl = pl.ds(pl.multiple_of(r * rows, rows), rows)
        x = x_ref[sl, :]
        ms = jnp.mean(x * x, axis=-1, keepdims=True)
        dst_ref[sl, :] = (x * lax.rsqrt(ms + RMS_EPS) * g_ref[...]).astype(dst_ref.dtype)
        return c
    lax.fori_loop(0, x_ref.shape[0] // rows, body, 0)


def _norm_mm_kernel(x_ref, g_ref, w_ref, o_ref, h_ref):
    @pl.when(pl.program_id(1) == 0)
    def _():
        _rmsnorm_rows(h_ref, x_ref, g_ref, 128)
    o_ref[...] = jnp.dot(h_ref[...], w_ref[...], preferred_element_type=F32).astype(o_ref.dtype)


def _norm_matmul(x, g, w, tm, tn):
    n, d = x.shape
    m = w.shape[1]
    return pl.pallas_call(
        _norm_mm_kernel,
        grid=(n // tm, m // tn),
        in_specs=[pl.BlockSpec((tm, d), lambda i, j: (i, 0)),
                  pl.BlockSpec((1, d), lambda i, j: (0, 0)),
                  pl.BlockSpec((d, tn), lambda i, j: (0, j))],
        out_specs=pl.BlockSpec((tm, tn), lambda i, j: (i, j)),
        out_shape=jax.ShapeDtypeStruct((n, m), F32),
        scratch_shapes=[pltpu.VMEM((tm, d), BF16)],
        compiler_params=_cparams(("parallel", "arbitrary")),
        name="norm_matmul",
    )(x, g.reshape(1, d), w)


def _norm_glu_kernel(x_ref, g_ref, w1_ref, w3_ref, o_ref, h_ref):
    @pl.when(pl.program_id(1) == 0)
    def _():
        _rmsnorm_rows(h_ref, x_ref, g_ref, 128)
    h = h_ref[...]
    a = jnp.dot(h, w1_ref[...], preferred_element_type=F32)
    b = jnp.dot(h, w3_ref[...], preferred_element_type=F32)
    o_ref[...] = (a * (1.0 / (1.0 + jnp.exp(-a))) * b).astype(o_ref.dtype)


def _norm_glu(x, g, w1, w3, tm, tn):
    n, d = x.shape
    m = w1.shape[1]
    return pl.pallas_call(
        _norm_glu_kernel,
        grid=(n // tm, m // tn),
        in_specs=[pl.BlockSpec((tm, d), lambda i, j: (i, 0)),
                  pl.BlockSpec((1, d), lambda i, j: (0, 0)),
                  pl.BlockSpec((d, tn), lambda i, j: (0, j)),
                  pl.BlockSpec((d, tn), lambda i, j: (0, j))],
        out_specs=pl.BlockSpec((tm, tn), lambda i, j: (i, j)),
        out_shape=jax.ShapeDtypeStruct((n, m), BF16),
        scratch_shapes=[pltpu.VMEM((tm, d), BF16)],
        compiler_params=_cparams(("parallel", "arbitrary")),
        name="norm_glu",
    )(x, g.reshape(1, d), w1, w3)


def _mm_res_kernel(a_ref, w_ref, r_ref, o_ref):
    o_ref[...] = r_ref[...] + jnp.dot(a_ref[...], w_ref[...], preferred_element_type=F32)


def _matmul_residual(a, w, res, tm, tn):
    n, k = a.shape
    m = w.shape[1]
    return pl.pallas_call(
        _mm_res_kernel,
        grid=(n // tm, m // tn),
        in_specs=[pl.BlockSpec((tm, k), lambda i, j: (i, 0)),
                  pl.BlockSpec((k, tn), lambda i, j: (0, j)),
                  pl.BlockSpec((tm, tn), lambda i, j: (i, j))],
        out_specs=pl.BlockSpec((tm, tn), lambda i, j: (i, j)),
        out_shape=jax.ShapeDtypeStruct((n, m), F32),
        compiler_params=_cparams(("parallel", "arbitrary")),
        name="matmul_residual",
    )(a, w, res)


def _final_norm_kernel(x_ref, g_ref, o_ref):
    _rmsnorm_rows(o_ref, x_ref, g_ref, 128)


def _final_norm(x, g, tm):
    n, d = x.shape
    return pl.pallas_call(
        _final_norm_kernel,
        grid=(n // tm,),
        in_specs=[pl.BlockSpec((tm, d), lambda i: (i, 0)),
                  pl.BlockSpec((1, d), lambda i: (0, 0))],
        out_specs=pl.BlockSpec((tm, d), lambda i: (i, 0)),
        out_shape=jax.ShapeDtypeStruct((n, d), F32),
        compiler_params=_cparams(("parallel",)),
        name="final_norm",
    )(x, g.reshape(1, d))


def _split3_bf16(x):
    hi = x.astype(BF16)
    r1 = x - hi.astype(F32)
    mid = r1.astype(BF16)
    lo = (r1 - mid.astype(F32)).astype(BF16)
    return hi, mid, lo


def _gla_kernel(q_ref, k_ref, v_ref, g_ref, a_ref, wa2_ref, ba_ref, gn_ref, o_ref, st_ref,
                *, heads, dk, dv):
    tb = q_ref.shape[0]

    @pl.when(pl.program_id(1) == 0)
    def _():
        st_ref[...] = jnp.zeros_like(st_ref)

    z = jnp.dot(a_ref[...].astype(BF16), wa2_ref[...], preferred_element_type=F32) + ba_ref[...]
    la = (jnp.minimum(z, 0.0) - jnp.log1p(jnp.exp(-jnp.abs(z)))) * (1.0 / GLA_GATE_TAU)

    ri = lax.broadcasted_iota(jnp.int32, (CHUNK, CHUNK), 0)
    ci = lax.broadcasted_iota(jnp.int32, (CHUNK, CHUNK), 1)
    tri = jnp.where(ri >= ci, 1.0, 0.0).astype(BF16)
    q_scale = dk ** -0.5

    for c in range(tb // CHUNK):
        rs = slice(c * CHUNK, (c + 1) * CHUNK)
        hi, mid, lo = _split3_bf16(la[rs, :])
        lcum = (jnp.dot(tri, hi, preferred_element_type=F32)
                + jnp.dot(tri, mid, preferred_element_type=F32)
                + jnp.dot(tri, lo, preferred_element_type=F32))
        ltot = lcum[CHUNK - 1:CHUNK, :]
        kd = (k_ref[rs, :] * jnp.exp(ltot - lcum)).astype(BF16)
        dec = jnp.exp(ltot)
        qc = (q_ref[rs, :] * q_scale).astype(BF16)
        for h in range(heads):
            ks = slice(h * dk, (h + 1) * dk)
            vs = slice(h * dv, (h + 1) * dv)
            vt = v_ref[rs, vs].T.astype(BF16)
            s_new = st_ref[h] * dec[:, ks] + jnp.dot(vt, kd[:, ks], preferred_element_type=F32)
            st_ref[h] = s_new
            oh = lax.dot_general(qc[:, ks], s_new.astype(BF16), _NT,
                                 preferred_element_type=F32)
            ms = jnp.mean(oh * oh, axis=-1, keepdims=True)
            y = oh * lax.rsqrt(ms + RMS_EPS) * gn_ref[...]
            gate = g_ref[rs, vs]
            y = y * (gate * (1.0 / (1.0 + jnp.exp(-gate))))
            o_ref[rs, vs] = y.astype(o_ref.dtype)


def _gla_scan(proj, wa2p, ba, gn, batch, seq, d_qk, d_v, tb):
    n = proj.shape[0]
    nt = seq // tb
    heads = GLA_HEADS
    dk, dv = d_qk // heads, d_v // heads
    row = lambda b, t: b * nt + t
    kern = functools.partial(_gla_kernel, heads=heads, dk=dk, dv=dv)
    return pl.pallas_call(
        kern,
        grid=(batch, nt),
        in_specs=[pl.BlockSpec((tb, d_qk), lambda b, t: (row(b, t), 0)),
                  pl.BlockSpec((tb, d_qk), lambda b, t: (row(b, t), 1)),
                  pl.BlockSpec((tb, d_v), lambda b, t: (row(b, t), (2 * d_qk) // d_v)),
                  pl.BlockSpec((tb, d_v), lambda b, t: (row(b, t), (2 * d_qk + d_v) // d_v)),
                  pl.BlockSpec((tb, LANE), lambda b, t: (row(b, t), (2 * d_qk + 2 * d_v) // LANE)),
                  pl.BlockSpec((LANE, d_qk), lambda b, t: (0, 0)),
                  pl.BlockSpec((1, d_qk), lambda b, t: (0, 0)),
                  pl.BlockSpec((1, dv), lambda b, t: (0, 0))],
        out_specs=pl.BlockSpec((tb, d_v), lambda b, t: (row(b, t), 0)),
        out_shape=jax.ShapeDtypeStruct((n, d_v), BF16),
        scratch_shapes=[pltpu.VMEM((heads, dv, dk), F32)],
        compiler_params=_cparams(("arbitrary", "arbitrary")),
        name="gla_scan",
    )(proj, proj, proj, proj, proj, wa2p, ba.reshape(1, d_qk), gn.reshape(1, dv))


def _t5_bucket(rel):
    nb = REL_BUCKETS // 2
    max_exact = nb // 2
    ret = (rel > 0).astype(jnp.int32) * nb
    n = jnp.abs(rel)
    large = max_exact + (jnp.log(jnp.maximum(n, 1).astype(jnp.float32) / max_exact)
                         / math.log(REL_MAX_DIST / max_exact) * (nb - max_exact)).astype(jnp.int32)
    large = jnp.minimum(large, nb - 1)
    return ret + jnp.where(n < max_exact, n, large)


def _fold_rows(x, op):
    parts = [x[r * SUBLANE:(r + 1) * SUBLANE, :] for r in range(x.shape[0] // SUBLANE)]
    while len(parts) > 1:
        parts = [op(parts[i], parts[i + 1]) for i in range(0, len(parts), 2)]
    return parts[0]


def _dsa_kernel(q_ref, qi_ref, c_ref, ki_ref, wi_ref, kvn_ref, kin_ref, wuvt_ref, bktt_ref, rb_ref,
                o_ref,
                crow_ref, ccol_ref, kirow_ref, bias_ref, qst_ref, qist_ref, scf_ref, scn_ref,
                m_ref, l_ref, acc_ref, *, topk):
    b = pl.program_id(0)
    j = pl.program_id(1)
    H, QB, LAT = DSA_HEADS, Q_BLOCK, DSA_LATENT
    PAIR = 2 * QB
    far_bucket = REL_BUCKETS // 2 - 1

    @pl.when((b == 0) & (j == 0))
    def _init():
        crow_ref[...] = jnp.zeros_like(crow_ref)
        ccol_ref[...] = jnp.zeros_like(ccol_ref)
        kirow_ref[...] = jnp.zeros_like(kirow_ref)
        bk = bktt_ref[...]

        def head_bias(h, c):
            far = rb_ref[far_bucket, h]
            t = jnp.zeros(bk.shape, F32)
            for bb in range(REL_BUCKETS):
                t = jnp.where(bk == bb, rb_ref[bb, h] - far, t)
            bias_ref[h] = t
            return c
        lax.fori_loop(0, H, head_bias, 0)

    r_own = pl.multiple_of((j + 1) * QB, QB)
    c = c_ref[...]
    cn = c * lax.rsqrt(jnp.mean(c * c, axis=-1, keepdims=True) + RMS_EPS) * kvn_ref[...]
    crow_ref[pl.ds(r_own, QB), :] = cn.astype(BF16)
    ccol_ref[j + 1] = cn.T.astype(BF16)
    ki = ki_ref[...]
    kin = ki * lax.rsqrt(jnp.mean(ki * ki, axis=-1, keepdims=True) + RMS_EPS) * kin_ref[...]
    kirow_ref[pl.ds(r_own, QB), :] = kin.astype(BF16)

    for h in range(H):
        hs = slice(h * QB, (h + 1) * QB)
        qst_ref[:, hs] = (q_ref[:, h * LAT:(h + 1) * LAT] * (LAT ** -0.5)).T.astype(BF16)
        qist_ref[:, hs] = qi_ref[:, h * IDX_DIM:(h + 1) * IDX_DIM].T.astype(BF16)
    wit = (wi_ref[...] * (IDX_HEADS ** -0.5 * IDX_DIM ** -0.5)).T

    nf = lax.shift_right_logical(j + 2, 2)
    tq_chunk = lax.shift_right_logical(lax.broadcasted_iota(jnp.int32, (1, QB), 1), int(math.log2(CHUNK)))
    rowf = lax.broadcasted_iota(jnp.int32, (FAR_W, QB), 0)
    rown = lax.broadcasted_iota(jnp.int32, (NEAR_W, QB), 0)
    adm_n = (rown < CHUNK * (QB // CHUNK + 1 + tq_chunk)) & ((j >= 1) | (rown >= QB))

    def score_of(kib):
        s = None
        for p in range(H // 2):
            sc = jnp.dot(kib, qist_ref[:, p * PAIR:(p + 1) * PAIR], preferred_element_type=F32)
            for u in range(2):
                h = 2 * p + u
                t = jnp.maximum(sc[:, u * QB:(u + 1) * QB], 0.0) * wit[h:h + 1, :]
                s = t if s is None else s + t
        return s

    def far_rows(mb):
        return pl.ds(pl.multiple_of(QB + mb * FAR_W, QB), FAR_W)

    def far_score(mb, carry):
        mn, mx = carry
        s = score_of(kirow_ref[far_rows(mb), :])
        adm = rowf < (j - 1) * QB - mb * FAR_W
        scf_ref[mb] = jnp.where(adm, s, -jnp.inf)
        mx = jnp.maximum(mx, _fold_rows(jnp.where(adm, s, -jnp.inf), jnp.maximum))
        mn = jnp.minimum(mn, _fold_rows(jnp.where(adm, s, jnp.inf), jnp.minimum))
        return mn, mx

    mn, mx = lax.fori_loop(0, nf, far_score,
                           (jnp.full((SUBLANE, QB), jnp.inf, F32), jnp.full((SUBLANE, QB), -jnp.inf, F32)))
    r_near = pl.multiple_of(j * QB, QB)
    s = score_of(kirow_ref[pl.ds(r_near, NEAR_W), :])
    scn_ref[...] = jnp.where(adm_n, s, -jnp.inf)
    mx = jnp.maximum(mx, _fold_rows(jnp.where(adm_n, s, -jnp.inf), jnp.maximum))
    mn = jnp.minimum(mn, _fold_rows(jnp.where(adm_n, s, jnp.inf), jnp.minimum))
    mx = jnp.max(mx, axis=0, keepdims=True)
    mn = jnp.min(mn, axis=0, keepdims=True)

    kf = float(topk)

    def count_ge(mid):
        def fb(mb, acc):
            return acc + _fold_rows(jnp.where(scf_ref[mb] >= mid, 1.0, 0.0), jnp.add)
        acc = lax.fori_loop(0, nf, fb, jnp.zeros((SUBLANE, QB), F32))
        acc = acc + _fold_rows(jnp.where(scn_ref[...] >= mid, 1.0, 0.0), jnp.add)
        return jnp.sum(acc, axis=0, keepdims=True)

    def bis_step(lo, hi, cnt):
        mid = 0.5 * lo + 0.5 * hi
        cm = count_ge(mid)
        take = cm >= kf
        stuck = (mid <= lo) | (mid >= hi)
        lo2 = jnp.where(take, mid, lo)
        hi2 = jnp.where(take, hi, mid)
        cnt2 = jnp.where(take, cm, cnt)
        return lo2, hi2, cnt2, (cnt2 <= kf) | stuck

    def bis_cond(carry):
        it, _, _, _, pending = carry
        return (pending > 0.0) & (it < MAX_BISECT)

    def bis_body(carry):
        it, lo, hi, cnt, _ = carry
        done = None
        for _ in range(BISECT_UNROLL):
            lo, hi, cnt, done = bis_step(lo, hi, cnt)
        return it + BISECT_UNROLL, lo, hi, cnt, jnp.max(jnp.where(done, 0.0, 1.0))

    n_adm = ((2 * j + tq_chunk + 1) * CHUNK).astype(F32)
    pending0 = jnp.max(jnp.where(n_adm <= kf, 0.0, 1.0))
    _, lo, _, _, _ = lax.while_loop(bis_cond, bis_body, (jnp.int32(0), mn, mx, n_adm, pending0))

    m_ref[...] = jnp.full(m_ref.shape, MASK_NEG, F32)
    l_ref[...] = jnp.zeros_like(l_ref)
    acc_ref[...] = jnp.zeros_like(acc_ref)

    def attend(crow, ccol, madd, near):
        for p in range(H // 2):
            ps = slice(p * PAIR, (p + 1) * PAIR)
            lg = jnp.dot(crow, qst_ref[:, ps], preferred_element_type=F32)
            probs, alphas = [], []
            for u in range(2):
                h = 2 * p + u
                hs = slice(h * QB, (h + 1) * QB)
                sh = lg[:, u * QB:(u + 1) * QB] + madd
                if near:
                    sh = sh + bias_ref[h]
                m_old = m_ref[:, hs]
                m_new = jnp.maximum(m_old, jnp.max(sh, axis=0, keepdims=True))
                alpha = jnp.exp(m_old - m_new)
                pr = jnp.exp(sh - m_new)
                l_ref[:, hs] = alpha * l_ref[:, hs] + jnp.sum(pr, axis=0, keepdims=True)
                m_ref[:, hs] = m_new
                probs.append(pr.astype(BF16))
                alphas.append(alpha)
            pv = jnp.dot(ccol, jnp.concatenate(probs, axis=1), preferred_element_type=F32)
            acc_ref[:, ps] = acc_ref[:, ps] * jnp.concatenate(alphas, axis=1) + pv

    def far_attend(mb, carry):
        madd = jnp.where(scf_ref[mb] >= lo, 0.0, MASK_NEG)
        blk = 1 + mb * (FAR_W // QB)
        ccol = jnp.concatenate([ccol_ref[blk + i] for i in range(FAR_W // QB)], axis=1)
        attend(crow_ref[far_rows(mb), :], ccol, madd, False)
        return carry

    lax.fori_loop(0, nf, far_attend, 0)
    madd = jnp.where(scn_ref[...] >= lo, 0.0, MASK_NEG)
    ccol = jnp.concatenate([ccol_ref[j], ccol_ref[j + 1]], axis=1)
    attend(crow_ref[pl.ds(r_near, NEAR_W), :], ccol, madd, True)

    dvh = wuvt_ref.shape[1]
    for h in range(H):
        hs = slice(h * QB, (h + 1) * QB)
        oht = (acc_ref[:, hs] * (1.0 / l_ref[:, hs])).astype(BF16)
        out_t = jnp.dot(wuvt_ref[h], oht, preferred_element_type=F32)
        o_ref[:, h * dvh:(h + 1) * dvh] = out_t.T.astype(o_ref.dtype)


def _dsa_attention(proj, kvn, kin, wuvt, bktt, rel_bias, batch, seq, d_out):
    n = proj.shape[0]
    nq = seq // Q_BLOCK
    H, QB, LAT = DSA_HEADS, Q_BLOCK, DSA_LATENT
    topk = min(TOPK_MAX, seq // 4)
    row = lambda b, j: b * nq + j
    o_qi = H * LAT
    o_c = o_qi + IDX_HEADS * IDX_DIM
    o_ki = o_c + LAT
    o_wi = o_ki + IDX_DIM
    n_far = max(1, (nq + 1) // 4)
    const = lambda b, j: (0, 0)
    return pl.pallas_call(
        functools.partial(_dsa_kernel, topk=topk),
        grid=(batch, nq),
        in_specs=[pl.BlockSpec((QB, H * LAT), lambda b, j: (row(b, j), 0)),
                  pl.BlockSpec((QB, IDX_HEADS * IDX_DIM), lambda b, j: (row(b, j), o_qi // (IDX_HEADS * IDX_DIM))),
                  pl.BlockSpec((QB, LAT), lambda b, j: (row(b, j), o_c // LAT)),
                  pl.BlockSpec((QB, IDX_DIM), lambda b, j: (row(b, j), o_ki // IDX_DIM)),
                  pl.BlockSpec((QB, LANE), lambda b, j: (row(b, j), o_wi // LANE)),
                  pl.BlockSpec((1, LAT), const),
                  pl.BlockSpec((1, IDX_DIM), const),
                  pl.BlockSpec(wuvt.shape, lambda b, j: (0, 0, 0)),
                  pl.BlockSpec((NEAR_W, QB), const),
                  pl.BlockSpec(memory_space=pltpu.SMEM)],
        out_specs=pl.BlockSpec((QB, d_out), lambda b, j: (row(b, j), 0)),
        out_shape=jax.ShapeDtypeStruct((n, d_out), BF16),
        scratch_shapes=[pltpu.VMEM((seq + QB, LAT), BF16),
                        pltpu.VMEM((nq + 1, LAT, QB), BF16),
                        pltpu.VMEM((seq + QB, IDX_DIM), BF16),
                        pltpu.VMEM((H, NEAR_W, QB), F32),
                        pltpu.VMEM((LAT, H * QB), BF16),
                        pltpu.VMEM((IDX_DIM, H * QB), BF16),
                        pltpu.VMEM((n_far, FAR_W, QB), F32),
                        pltpu.VMEM((NEAR_W, QB), F32),
                        pltpu.VMEM((1, H * QB), F32),
                        pltpu.VMEM((1, H * QB), F32),
                        pltpu.VMEM((LAT, H * QB), F32)],
        compiler_params=_cparams(("arbitrary", "arbitrary")),
        name="dsa_attention",
    )(proj, proj, proj, proj, proj, kvn.reshape(1, LAT), kin.reshape(1, IDX_DIM), wuvt, bktt, rel_bias)


def _pad_cols(w, mult):
    pad = (-w.shape[1]) % mult
    return jnp.pad(w, ((0, 0), (0, pad))) if pad else w


def kernel(x, norm_mix, norm_ffn, norm_final, gla_w_in, gla_w_a2, gla_b_a, gla_g_norm, gla_w_out,
           dsa_w_in, dsa_kv_norm, dsa_kidx_norm, dsa_w_uv, dsa_w_out, rel_bias, ffn_w1, ffn_w3, ffn_w2):
    batch, seq, d = x.shape
    depth = norm_mix.shape[0]
    n = batch * seq
    d_qk = gla_w_a2.shape[2]
    d_v = gla_w_out.shape[1]
    tn = 512
    tm = min(1024, n)

    wk = jnp.arange(NEAR_W, dtype=jnp.int32)[:, None]
    tq = jnp.arange(Q_BLOCK, dtype=jnp.int32)[None, :]
    bktt = _t5_bucket(wk - Q_BLOCK - tq)

    s1 = DSA_HEADS * DSA_LATENT
    s2 = s1 + DSA_LATENT
    s3 = s2 + IDX_HEADS * IDX_DIM
    s4 = s3 + IDX_DIM

    xf = x.reshape(n, d)
    for i in range(depth):
        jm = i // 2
        if i % 2 == 0:
            w_in = _pad_cols(gla_w_in[jm].astype(BF16), tn)
            wa2p = jnp.pad(gla_w_a2[jm].astype(BF16), ((0, LANE - GLA_GATE_RANK), (0, 0)))
            proj = _norm_matmul(xf, norm_mix[i], w_in, tm, tn)
            u = _gla_scan(proj, wa2p, gla_b_a[jm], gla_g_norm[jm], batch, seq, d_qk, d_v, min(256, seq))
            xf = _matmul_residual(u, gla_w_out[jm].astype(BF16), xf, tm, tn)
        else:
            w = dsa_w_in[jm]
            w_in = jnp.concatenate([w[:, :s1], w[:, s2:s3], w[:, s1:s2], w[:, s3:s4], w[:, s4:]], axis=1)
            w_in = _pad_cols(w_in.astype(BF16), tn)
            proj = _norm_matmul(xf, norm_mix[i], w_in, tm, tn)
            wuvt = jnp.swapaxes(dsa_w_uv[jm], 1, 2).astype(BF16)
            u = _dsa_attention(proj, dsa_kv_norm[jm], dsa_kidx_norm[jm], wuvt,
                               bktt, rel_bias, batch, seq, dsa_w_out.shape[1])
            xf = _matmul_residual(u, dsa_w_out[jm].astype(BF16), xf, tm, tn)
        act = _norm_glu(xf, norm_ffn[i], ffn_w1[i].astype(BF16), ffn_w3[i].astype(BF16), tm, tn)
        xf = _matmul_residual(act, ffn_w2[i].astype(BF16), xf, min(512, n), tn)
    return _final_norm(xf, norm_final, min(512, n)).reshape(batch, seq, d)
```

```python
import functools
import math

import jax
import jax.numpy as jnp
from jax import lax
from jax.experimental import pallas as pl
from jax.experimental.pallas import tpu as pltpu

F32 = jnp.float32
BF16 = jnp.bfloat16

RMS_EPS = 1e-6
CHUNK = 64

GLA_HEADS = 4
GLA_GATE_RANK = 16
GLA_GATE_TAU = 16.0

DSA_HEADS = 16
DSA_LATENT = 256
IDX_HEADS = 16
IDX_DIM = 128
TOPK_MAX = 256
Q_BLOCK = 128

REL_BUCKETS = 32
REL_MAX_DIST = 128

LANE = 128
SUBLANE = 8
FAR_W = 512
NEAR_W = 2 * Q_BLOCK
MASK_NEG = -1e30
BISECT_UNROLL = 4
MAX_BISECT = 200
VMEM_LIMIT = 56 * 1024 * 1024

_NT = (((1,), (1,)), ((), ()))


def _cparams(sem):
    return pltpu.CompilerParams(dimension_semantics=sem, vmem_limit_bytes=VMEM_LIMIT)


def _rmsnorm_rows(dst_ref, x_ref, g_ref, rows):
    def body(r, c):
        sl = pl.ds(pl.multiple_of(r * rows, rows), rows)
        x = x_ref[sl, :]
        ms = jnp.mean(x * x, axis=-1, keepdims=True)
        dst_ref[sl, :] = (x * lax.rsqrt(ms + RMS_EPS) * g_ref[...]).astype(dst_ref.dtype)
        return c
    lax.fori_loop(0, x_ref.shape[0] // rows, body, 0)


def _dot_cast_w(h_ref, w_ref, kc=256):
    acc = None
    for k0 in range(0, w_ref.shape[0], kc):
        part = jnp.dot(h_ref[:, k0:k0 + kc], w_ref[k0:k0 + kc, :].astype(BF16),
                       preferred_element_type=F32)
        acc = part if acc is None else acc + part
    return acc


def _cast_kernel(w_ref, o_ref, *, valid_cols):
    w = w_ref[...]
    if valid_cols is not None:
        col = pl.program_id(1) * w.shape[1] + lax.broadcasted_iota(jnp.int32, w.shape, 1)
        w = jnp.where(col < valid_cols, w, 0.0)
    o_ref[...] = w.astype(o_ref.dtype)


def _cast_weights(w, tn):
    nl, k, m = w.shape
    nj = pl.cdiv(m, tn)
    return pl.pallas_call(
        functools.partial(_cast_kernel, valid_cols=None if m % tn == 0 else m),
        grid=(nl, nj),
        in_specs=[pl.BlockSpec((None, k, tn), lambda l, j: (l, 0, j))],
        out_specs=pl.BlockSpec((None, k, tn), lambda l, j: (l, 0, j)),
        out_shape=jax.ShapeDtypeStruct((nl, k, nj * tn), BF16),
        compiler_params=_cparams(("parallel", "parallel")),
        name="cast_weights",
    )(w)


def _norm_mm_kernel(x_ref, g_ref, w_ref, o_ref, h_ref):
    @pl.when(pl.program_id(1) == 0)
    def _():
        _rmsnorm_rows(h_ref, x_ref, g_ref, 128)
    o_ref[...] = jnp.dot(h_ref[...], w_ref[...], preferred_element_type=F32).astype(o_ref.dtype)


def _norm_matmul(x, g, w, layer, tm, tn):
    n, d = x.shape
    m = w.shape[2]
    return pl.pallas_call(
        _norm_mm_kernel,
        grid=(n // tm, m // tn),
        in_specs=[pl.BlockSpec((tm, d), lambda i, j: (i, 0)),
                  pl.BlockSpec((1, d), lambda i, j: (0, 0)),
                  pl.BlockSpec((None, d, tn), lambda i, j: (layer, 0, j))],
        out_specs=pl.BlockSpec((tm, tn), lambda i, j: (i, j)),
        out_shape=jax.ShapeDtypeStruct((n, m), BF16),
        scratch_shapes=[pltpu.VMEM((tm, d), BF16)],
        compiler_params=_cparams(("parallel", "arbitrary")),
        name="norm_matmul",
    )(x, g.reshape(1, d), w)


def _glu_kernel(h_ref, w1_ref, w3_ref, o_ref):
    a = _dot_cast_w(h_ref, w1_ref)
    b = _dot_cast_w(h_ref, w3_ref)
    o_ref[...] = (a * (1.0 / (1.0 + jnp.exp(-a))) * b).astype(o_ref.dtype)


def _glu(h, w1, w3, layer, tm, tn):
    n, d = h.shape
    m = w1.shape[2]
    return pl.pallas_call(
        _glu_kernel,
        grid=(n // tm, m // tn),
        in_specs=[pl.BlockSpec((tm, d), lambda i, j: (i, 0)),
                  pl.BlockSpec((None, d, tn), lambda i, j: (layer, 0, j)),
                  pl.BlockSpec((None, d, tn), lambda i, j: (layer, 0, j))],
        out_specs=pl.BlockSpec((tm, tn), lambda i, j: (i, j)),
        out_shape=jax.ShapeDtypeStruct((n, m), BF16),
        compiler_params=_cparams(("parallel", "arbitrary")),
        name="glu",
    )(h, w1, w3)


def _out_norm_kernel(a_ref, w_ref, r_ref, g_ref, x_ref, h_ref):
    x_ref[...] = r_ref[...] + jnp.dot(a_ref[...], w_ref[...], preferred_element_type=F32)
    _rmsnorm_rows(h_ref, x_ref, g_ref, 128)


def _out_proj_norm(a, w, res, g, tm):
    n, k = a.shape
    d = w.shape[1]
    return pl.pallas_call(
        _out_norm_kernel,
        grid=(n // tm,),
        in_specs=[pl.BlockSpec((tm, k), lambda i: (i, 0)),
                  pl.BlockSpec((k, d), lambda i: (0, 0)),
                  pl.BlockSpec((tm, d), lambda i: (i, 0)),
                  pl.BlockSpec((1, d), lambda i: (0, 0))],
        out_specs=[pl.BlockSpec((tm, d), lambda i: (i, 0)),
                   pl.BlockSpec((tm, d), lambda i: (i, 0))],
        out_shape=[jax.ShapeDtypeStruct((n, d), F32), jax.ShapeDtypeStruct((n, d), BF16)],
        compiler_params=_cparams(("parallel",)),
        name="out_proj_norm",
    )(a, w, res, g.reshape(1, d))


def _mm_res_kernel(a_ref, w_ref, r_ref, o_ref):
    o_ref[...] = r_ref[...] + jnp.dot(a_ref[...], w_ref[...], preferred_element_type=F32)


def _matmul_residual(a, w, layer, res, tm, tn):
    n, k = a.shape
    m = w.shape[2]
    return pl.pallas_call(
        _mm_res_kernel,
        grid=(n // tm, m // tn),
        in_specs=[pl.BlockSpec((tm, k), lambda i, j: (i, 0)),
                  pl.BlockSpec((None, k, tn), lambda i, j: (layer, 0, j)),
                  pl.BlockSpec((tm, tn), lambda i, j: (i, j))],
        out_specs=pl.BlockSpec((tm, tn), lambda i, j: (i, j)),
        out_shape=jax.ShapeDtypeStruct((n, m), F32),
        compiler_params=_cparams(("parallel", "arbitrary")),
        name="matmul_residual",
    )(a, w, res)


def _final_norm_kernel(x_ref, g_ref, o_ref):
    _rmsnorm_rows(o_ref, x_ref, g_ref, 128)


def _final_norm(x, g, tm):
    n, d = x.shape
    return pl.pallas_call(
        _final_norm_kernel,
        grid=(n // tm,),
        in_specs=[pl.BlockSpec((tm, d), lambda i: (i, 0)),
                  pl.BlockSpec((1, d), lambda i: (0, 0))],
        out_specs=pl.BlockSpec((tm, d), lambda i: (i, 0)),
        out_shape=jax.ShapeDtypeStruct((n, d), F32),
        compiler_params=_cparams(("parallel",)),
        name="final_norm",
    )(x, g.reshape(1, d))


def _split3_bf16(x):
    hi = x.astype(BF16)
    r1 = x - hi.astype(F32)
    mid = r1.astype(BF16)
    lo = (r1 - mid.astype(F32)).astype(BF16)
    return hi, mid, lo


def _gla_kernel(q_ref, k_ref, v_ref, g_ref, a_ref, wa2_ref, ba_ref, gn_ref, o_ref, st_ref,
                *, heads, dk, dv):
    tb = q_ref.shape[0]

    @pl.when(pl.program_id(1) == 0)
    def _():
        st_ref[...] = jnp.zeros_like(st_ref)

    z = jnp.dot(a_ref[...].astype(BF16), wa2_ref[...], preferred_element_type=F32) + ba_ref[...]
    la = (jnp.minimum(z, 0.0) - jnp.log1p(jnp.exp(-jnp.abs(z)))) * (1.0 / GLA_GATE_TAU)

    ri = lax.broadcasted_iota(jnp.int32, (CHUNK, CHUNK), 0)
    ci = lax.broadcasted_iota(jnp.int32, (CHUNK, CHUNK), 1)
    tri = jnp.where(ri >= ci, 1.0, 0.0).astype(BF16)
    q_scale = dk ** -0.5

    for c in range(tb // CHUNK):
        rs = slice(c * CHUNK, (c + 1) * CHUNK)
        hi, mid, lo = _split3_bf16(la[rs, :])
        lcum = (jnp.dot(tri, hi, preferred_element_type=F32)
                + jnp.dot(tri, mid, preferred_element_type=F32)
                + jnp.dot(tri, lo, preferred_element_type=F32))
        ltot = lcum[CHUNK - 1:CHUNK, :]
        kd = (k_ref[rs, :] * jnp.exp(ltot - lcum)).astype(BF16)
        dec = jnp.exp(ltot)
        qc = (q_ref[rs, :] * q_scale).astype(BF16)
        for h in range(heads):
            ks = slice(h * dk, (h + 1) * dk)
            vs = slice(h * dv, (h + 1) * dv)
            vt = v_ref[rs, vs].T.astype(BF16)
            s_new = st_ref[h] * dec[:, ks] + jnp.dot(vt, kd[:, ks], preferred_element_type=F32)
            st_ref[h] = s_new
            oh = lax.dot_general(qc[:, ks], s_new.astype(BF16), _NT,
                                 preferred_element_type=F32)
            ms = jnp.mean(oh * oh, axis=-1, keepdims=True)
            y = oh * lax.rsqrt(ms + RMS_EPS) * gn_ref[...]
            gate = g_ref[rs, vs].astype(F32)
            y = y * (gate * (1.0 / (1.0 + jnp.exp(-gate))))
            o_ref[rs, vs] = y.astype(o_ref.dtype)


def _gla_scan(proj, wa2p, ba, gn, batch, seq, d_qk, d_v, tb):
    n = proj.shape[0]
    nt = seq // tb
    heads = GLA_HEADS
    dk, dv = d_qk // heads, d_v // heads
    row = lambda b, t: b * nt + t
    kern = functools.partial(_gla_kernel, heads=heads, dk=dk, dv=dv)
    return pl.pallas_call(
        kern,
        grid=(batch, nt),
        in_specs=[pl.BlockSpec((tb, d_qk), lambda b, t: (row(b, t), 0)),
                  pl.BlockSpec((tb, d_qk), lambda b, t: (row(b, t), 1)),
                  pl.BlockSpec((tb, d_v), lambda b, t: (row(b, t), (2 * d_qk) // d_v)),
                  pl.BlockSpec((tb, d_v), lambda b, t: (row(b, t), (2 * d_qk + d_v) // d_v)),
                  pl.BlockSpec((tb, LANE), lambda b, t: (row(b, t), (2 * d_qk + 2 * d_v) // LANE)),
                  pl.BlockSpec((LANE, d_qk), lambda b, t: (0, 0)),
                  pl.BlockSpec((1, d_qk), lambda b, t: (0, 0)),
                  pl.BlockSpec((1, dv), lambda b, t: (0, 0))],
        out_specs=pl.BlockSpec((tb, d_v), lambda b, t: (row(b, t), 0)),
        out_shape=jax.ShapeDtypeStruct((n, d_v), BF16),
        scratch_shapes=[pltpu.VMEM((heads, dv, dk), F32)],
        compiler_params=_cparams(("arbitrary", "arbitrary")),
        name="gla_scan",
    )(proj, proj, proj, proj, proj, wa2p, ba.reshape(1, d_qk), gn.reshape(1, dv))


def _t5_bucket(rel):
    nb = REL_BUCKETS // 2
    max_exact = nb // 2
    ret = (rel > 0).astype(jnp.int32) * nb
    n = jnp.abs(rel)
    large = max_exact + (jnp.log(jnp.maximum(n, 1).astype(jnp.float32) / max_exact)
                         / math.log(REL_MAX_DIST / max_exact) * (nb - max_exact)).astype(jnp.int32)
    large = jnp.minimum(large, nb - 1)
    return ret + jnp.where(n < max_exact, n, large)


def _fold_rows(x, op):
    parts = [x[r * SUBLANE:(r + 1) * SUBLANE, :] for r in range(x.shape[0] // SUBLANE)]
    while len(parts) > 1:
        parts = [op(parts[i], parts[i + 1]) for i in range(0, len(parts), 2)]
    return parts[0]


def _dsa_kernel(*refs, topk):
    npair = IDX_HEADS // 2
    q_ref, c_ref, ki_ref, wi_ref = refs[:4]
    qip_refs = refs[4:4 + npair]
    (kvn_ref, kin_ref, wuvt_ref, bktt_ref, rb_ref, o_ref,
     crow_ref, ccol_ref, kirow_ref, bias_ref, qst_ref, qist_ref, scf_ref, scn_ref,
     m_ref, l_ref, acc_ref) = refs[4 + npair:]
    b = pl.program_id(0)
    j = pl.program_id(1)
    H, QB, LAT = DSA_HEADS, Q_BLOCK, DSA_LATENT
    PAIR = 2 * QB
    far_bucket = REL_BUCKETS // 2 - 1

    @pl.when((b == 0) & (j == 0))
    def _init():
        crow_ref[...] = jnp.zeros_like(crow_ref)
        ccol_ref[...] = jnp.zeros_like(ccol_ref)
        kirow_ref[...] = jnp.zeros_like(kirow_ref)
        bk = bktt_ref[...]

        def head_bias(h, c):
            far = rb_ref[far_bucket, h]
            t = jnp.zeros(bk.shape, F32)
            for bb in range(REL_BUCKETS):
                t = jnp.where(bk == bb, rb_ref[bb, h] - far, t)
            bias_ref[h] = t
            return c
        lax.fori_loop(0, H, head_bias, 0)

    r_own = pl.multiple_of((j + 1) * QB, QB)
    c = c_ref[...].astype(F32)
    cn = c * lax.rsqrt(jnp.mean(c * c, axis=-1, keepdims=True) + RMS_EPS) * kvn_ref[...]
    crow_ref[pl.ds(r_own, QB), :] = cn.astype(BF16)
    ccol_ref[j + 1] = cn.T.astype(BF16)
    ki = ki_ref[...].astype(F32)
    kin = ki * lax.rsqrt(jnp.mean(ki * ki, axis=-1, keepdims=True) + RMS_EPS) * kin_ref[...]
    kirow_ref[pl.ds(r_own, QB), :] = kin.astype(BF16)

    for h in range(H):
        hs = slice(h * QB, (h + 1) * QB)
        qst_ref[:, hs] = (q_ref[:, h * LAT:(h + 1) * LAT] * (LAT ** -0.5)).T
        qist_ref[:, hs] = qip_refs[h // 2][:, (h % 2) * IDX_DIM:(h % 2 + 1) * IDX_DIM].T
    wit = (wi_ref[...].astype(F32) * (IDX_HEADS ** -0.5 * IDX_DIM ** -0.5)).T

    nf = lax.shift_right_logical(j + 2, 2)
    tq_chunk = lax.shift_right_logical(lax.broadcasted_iota(jnp.int32, (1, QB), 1), int(math.log2(CHUNK)))
    rowf = lax.broadcasted_iota(jnp.int32, (FAR_W, QB), 0)
    rown = lax.broadcasted_iota(jnp.int32, (NEAR_W, QB), 0)
    adm_n = (rown < CHUNK * (QB // CHUNK + 1 + tq_chunk)) & ((j >= 1) | (rown >= QB))

    def score_of(kib):
        s = None
        for p in range(H // 2):
            sc = jnp.dot(kib, qist_ref[:, p * PAIR:(p + 1) * PAIR], preferred_element_type=F32)
            for u in range(2):
                h = 2 * p + u
                t = jnp.maximum(sc[:, u * QB:(u + 1) * QB], 0.0) * wit[h:h + 1, :]
                s = t if s is None else s + t
        return s

    def far_rows(mb):
        return pl.ds(pl.multiple_of(QB + mb * FAR_W, QB), FAR_W)

    def far_score(mb, carry):
        mn, mx = carry
        s = score_of(kirow_ref[far_rows(mb), :])
        adm = rowf < (j - 1) * QB - mb * FAR_W
        scf_ref[mb] = jnp.where(adm, s, -jnp.inf)
        mx = jnp.maximum(mx, _fold_rows(jnp.where(adm, s, -jnp.inf), jnp.maximum))
        mn = jnp.minimum(mn, _fold_rows(jnp.where(adm, s, jnp.inf), jnp.minimum))
        return mn, mx

    mn, mx = lax.fori_loop(0, nf, far_score,
                           (jnp.full((SUBLANE, QB), jnp.inf, F32), jnp.full((SUBLANE, QB), -jnp.inf, F32)))
    r_near = pl.multiple_of(j * QB, QB)
    s = score_of(kirow_ref[pl.ds(r_near, NEAR_W), :])
    scn_ref[...] = jnp.where(adm_n, s, -jnp.inf)
    mx = jnp.maximum(mx, _fold_rows(jnp.where(adm_n, s, -jnp.inf), jnp.maximum))
    mn = jnp.minimum(mn, _fold_rows(jnp.where(adm_n, s, jnp.inf), jnp.minimum))
    mx = jnp.max(mx, axis=0, keepdims=True)
    mn = jnp.min(mn, axis=0, keepdims=True)

    kf = float(topk)

    def count_ge(mid):
        def fb(mb, acc):
            return acc + _fold_rows(jnp.where(scf_ref[mb] >= mid, 1.0, 0.0), jnp.add)
        acc = lax.fori_loop(0, nf, fb, jnp.zeros((SUBLANE, QB), F32))
        acc = acc + _fold_rows(jnp.where(scn_ref[...] >= mid, 1.0, 0.0), jnp.add)
        return jnp.sum(acc, axis=0, keepdims=True)

    def bis_step(lo, hi, cnt):
        mid = 0.5 * lo + 0.5 * hi
        cm = count_ge(mid)
        take = cm >= kf
        stuck = (mid <= lo) | (mid >= hi)
        lo2 = jnp.where(take, mid, lo)
        hi2 = jnp.where(take, hi, mid)
        cnt2 = jnp.where(take, cm, cnt)
        return lo2, hi2, cnt2, (cnt2 <= kf) | stuck

    def bis_cond(carry):
        it, _, _, _, pending = carry
        return (pending > 0.0) & (it < MAX_BISECT)

    def bis_body(carry):
        it, lo, hi, cnt, _ = carry
        done = None
        for _ in range(BISECT_UNROLL):
            lo, hi, cnt, done = bis_step(lo, hi, cnt)
        return it + BISECT_UNROLL, lo, hi, cnt, jnp.max(jnp.where(done, 0.0, 1.0))

    n_adm = ((2 * j + tq_chunk + 1) * CHUNK).astype(F32)
    pending0 = jnp.max(jnp.where(n_adm <= kf, 0.0, 1.0))
    _, lo, _, _, _ = lax.while_loop(bis_cond, bis_body, (jnp.int32(0), mn, mx, n_adm, pending0))

    m_ref[...] = jnp.full(m_ref.shape, MASK_NEG, F32)
    l_ref[...] = jnp.zeros_like(l_ref)
    acc_ref[...] = jnp.zeros_like(acc_ref)

    def attend(crow, ccol, madd, near):
        for p in range(H // 2):
            ps = slice(p * PAIR, (p + 1) * PAIR)
            lg = jnp.dot(crow, qst_ref[:, ps], preferred_element_type=F32)
            probs, alphas = [], []
            for u in range(2):
                h = 2 * p + u
                hs = slice(h * QB, (h + 1) * QB)
                sh = lg[:, u * QB:(u + 1) * QB] + madd
                if near:
                    sh = sh + bias_ref[h]
                m_old = m_ref[:, hs]
                m_new = jnp.maximum(m_old, jnp.max(sh, axis=0, keepdims=True))
                alpha = jnp.exp(m_old - m_new)
                pr = jnp.exp(sh - m_new)
                l_ref[:, hs] = alpha * l_ref[:, hs] + jnp.sum(pr, axis=0, keepdims=True)
                m_ref[:, hs] = m_new
                probs.append(pr.astype(BF16))
                alphas.append(alpha)
            pv = jnp.dot(ccol, jnp.concatenate(probs, axis=1), preferred_element_type=F32)
            acc_ref[:, ps] = acc_ref[:, ps] * jnp.concatenate(alphas, axis=1) + pv

    def far_attend(mb, carry):
        madd = jnp.where(scf_ref[mb] >= lo, 0.0, MASK_NEG)
        blk = 1 + mb * (FAR_W // QB)
        ccol = jnp.concatenate([ccol_ref[blk + i] for i in range(FAR_W // QB)], axis=1)
        attend(crow_ref[far_rows(mb), :], ccol, madd, False)
        return carry

    lax.fori_loop(0, nf, far_attend, 0)
    madd = jnp.where(scn_ref[...] >= lo, 0.0, MASK_NEG)
    ccol = jnp.concatenate([ccol_ref[j], ccol_ref[j + 1]], axis=1)
    attend(crow_ref[pl.ds(r_near, NEAR_W), :], ccol, madd, True)

    dvh = wuvt_ref.shape[1]
    for h in range(H):
        hs = slice(h * QB, (h + 1) * QB)
        oht = (acc_ref[:, hs] * (1.0 / l_ref[:, hs])).astype(BF16)
        out_t = jnp.dot(wuvt_ref[h], oht, preferred_element_type=F32)
        o_ref[:, h * dvh:(h + 1) * dvh] = out_t.T.astype(o_ref.dtype)


def _dsa_attention(proj, kvn, kin, wuvt, bktt, rel_bias, batch, seq, d_out):
    n = proj.shape[0]
    nq = seq // Q_BLOCK
    H, QB, LAT = DSA_HEADS, Q_BLOCK, DSA_LATENT
    topk = min(TOPK_MAX, seq // 4)
    row = lambda b, j: b * nq + j
    o_c = H * LAT
    o_qi = o_c + LAT
    o_ki = o_qi + IDX_HEADS * IDX_DIM
    o_wi = o_ki + IDX_DIM
    pair_w = 2 * IDX_DIM
    n_far = max(1, (nq + 1) // 4)
    const = lambda b, j: (0, 0)
    return pl.pallas_call(
        functools.partial(_dsa_kernel, topk=topk),
        grid=(batch, nq),
        in_specs=[pl.BlockSpec((QB, H * LAT), lambda b, j: (row(b, j), 0)),
                  pl.BlockSpec((QB, LAT), lambda b, j: (row(b, j), o_c // LAT)),
                  pl.BlockSpec((QB, IDX_DIM), lambda b, j: (row(b, j), o_ki // IDX_DIM)),
                  pl.BlockSpec((QB, LANE), lambda b, j: (row(b, j), o_wi // LANE))]
                 + [pl.BlockSpec((QB, pair_w), functools.partial(
                     lambda b, j, p: (row(b, j), o_qi // pair_w + p), p=p)) for p in range(IDX_HEADS // 2)]
                 + [pl.BlockSpec((1, LAT), const),
                    pl.BlockSpec((1, IDX_DIM), const),
                    pl.BlockSpec(wuvt.shape, lambda b, j: (0, 0, 0)),
                    pl.BlockSpec((NEAR_W, QB), const),
                    pl.BlockSpec(memory_space=pltpu.SMEM)],
        out_specs=pl.BlockSpec((QB, d_out), lambda b, j: (row(b, j), 0)),
        out_shape=jax.ShapeDtypeStruct((n, d_out), BF16),
        scratch_shapes=[pltpu.VMEM((seq + QB, LAT), BF16),
                        pltpu.VMEM((nq + 1, LAT, QB), BF16),
                        pltpu.VMEM((seq + QB, IDX_DIM), BF16),
                        pltpu.VMEM((H, NEAR_W, QB), F32),
                        pltpu.VMEM((LAT, H * QB), BF16),
                        pltpu.VMEM((IDX_DIM, H * QB), BF16),
                        pltpu.VMEM((n_far, FAR_W, QB), F32),
                        pltpu.VMEM((NEAR_W, QB), F32),
                        pltpu.VMEM((1, H * QB), F32),
                        pltpu.VMEM((1, H * QB), F32),
                        pltpu.VMEM((LAT, H * QB), F32)],
        compiler_params=_cparams(("arbitrary", "arbitrary")),
        name="dsa_attention",
    )(*([proj] * (4 + IDX_HEADS // 2)), kvn.reshape(1, LAT), kin.reshape(1, IDX_DIM), wuvt, bktt, rel_bias)


def kernel(x, norm_mix, norm_ffn, norm_final, gla_w_in, gla_w_a2, gla_b_a, gla_g_norm, gla_w_out,
           dsa_w_in, dsa_kv_norm, dsa_kidx_norm, dsa_w_uv, dsa_w_out, rel_bias, ffn_w1, ffn_w3, ffn_w2):
    batch, seq, d = x.shape
    depth = norm_mix.shape[0]
    n = batch * seq
    d_qk = gla_w_a2.shape[2]
    d_v = gla_w_out.shape[1]
    tn = 512

    wk = jnp.arange(NEAR_W, dtype=jnp.int32)[:, None]
    tq = jnp.arange(Q_BLOCK, dtype=jnp.int32)[None, :]
    bktt = _t5_bucket(wk - Q_BLOCK - tq)

    gla_win = _cast_weights(gla_w_in, tn)
    dsa_win = _cast_weights(dsa_w_in, tn)
    w2 = _cast_weights(ffn_w2, tn)
    gla_wout = gla_w_out.astype(BF16)
    dsa_wout = dsa_w_out.astype(BF16)
    wa2p = jnp.pad(gla_w_a2.astype(BF16), ((0, 0), (0, LANE - GLA_GATE_RANK), (0, 0)))
    wuvt = jnp.swapaxes(dsa_w_uv, 2, 3).astype(BF16)

    xf = x.reshape(n, d)
    for i in range(depth):
        jm = i // 2
        if i % 2 == 0:
            proj = _norm_matmul(xf, norm_mix[i], gla_win, jm, min(1024, n), tn)
            u = _gla_scan(proj, wa2p[jm], gla_b_a[jm], gla_g_norm[jm], batch, seq, d_qk, d_v, min(256, seq))
            xf, h = _out_proj_norm(u, gla_wout[jm], xf, norm_ffn[i], min(512, n))
        else:
            proj = _norm_matmul(xf, norm_mix[i], dsa_win, jm, min(1024, n), tn)
            u = _dsa_attention(proj, dsa_kv_norm[jm], dsa_kidx_norm[jm], wuvt[jm],
                               bktt, rel_bias, batch, seq, dsa_w_out.shape[1])
            xf, h = _out_proj_norm(u, dsa_wout[jm], xf, norm_ffn[i], min(512, n))
        act = _glu(h, ffn_w1, ffn_w3, i, min(2048, n), tn)
        xf = _matmul_residual(act, w2, i, xf, min(1024, n), tn)
    return _final_norm(xf, norm_final, min(512, n)).reshape(batch, seq, d)
```

```python
import functools
import math

import jax
import jax.numpy as jnp
from jax import lax
from jax.experimental import pallas as pl
from jax.experimental.pallas import tpu as pltpu

F32 = jnp.float32
BF16 = jnp.bfloat16

RMS_EPS = 1e-6
CHUNK = 64

GLA_HEADS = 4
GLA_GATE_RANK = 16
GLA_GATE_TAU = 16.0

DSA_HEADS = 16
DSA_LATENT = 256
IDX_HEADS = 16
IDX_DIM = 128
TOPK_MAX = 256
Q_BLOCK = 128

REL_BUCKETS = 32
REL_MAX_DIST = 128

LANE = 128
SUBLANE = 8
FAR_W = 512
NEAR_W = 2 * Q_BLOCK
MASK_NEG = -1e30
LOG2E = math.log2(math.e)
ATT_GROUP = 2
ONES_ROWS = 16
BISECT_UNROLL = 4
MAX_BISECT = 200
VMEM_LIMIT = 56 * 1024 * 1024

_NT = (((1,), (1,)), ((), ()))


def _cparams(sem):
    return pltpu.CompilerParams(dimension_semantics=sem, vmem_limit_bytes=VMEM_LIMIT)


def _rmsnorm_rows(dst_ref, x_ref, g_ref, rows):
    def body(r, c):
        sl = pl.ds(pl.multiple_of(r * rows, rows), rows)
        x = x_ref[sl, :]
        ms = jnp.mean(x * x, axis=-1, keepdims=True)
        dst_ref[sl, :] = (x * lax.rsqrt(ms + RMS_EPS) * g_ref[...]).astype(dst_ref.dtype)
        return c
    lax.fori_loop(0, x_ref.shape[0] // rows, body, 0)


def _dot_cast_w(h_ref, w_ref, kc=256):
    acc = None
    for k0 in range(0, w_ref.shape[0], kc):
        part = jnp.dot(h_ref[:, k0:k0 + kc], w_ref[k0:k0 + kc, :].astype(BF16),
                       preferred_element_type=F32)
        acc = part if acc is None else acc + part
    return acc


def _cast_kernel(w_ref, o_ref):
    o_ref[...] = w_ref[...].astype(o_ref.dtype)


def _cast_weights(w, tn):
    nl, k, m = w.shape
    return pl.pallas_call(
        _cast_kernel,
        grid=(nl, m // tn),
        in_specs=[pl.BlockSpec((None, k, tn), lambda l, j: (l, 0, j))],
        out_specs=pl.BlockSpec((None, k, tn), lambda l, j: (l, 0, j)),
        out_shape=jax.ShapeDtypeStruct((nl, k, m), BF16),
        compiler_params=_cparams(("parallel", "parallel")),
        name="cast_weights",
    )(w)


def _cast_pad_cols(w, mult):
    return jnp.pad(w.astype(BF16), ((0, 0), (0, 0), (0, (-w.shape[2]) % mult)))


def _norm_mm_kernel(x_ref, g_ref, w_ref, o_ref, h_ref):
    @pl.when(pl.program_id(1) == 0)
    def _():
        _rmsnorm_rows(h_ref, x_ref, g_ref, 128)
    o_ref[...] = jnp.dot(h_ref[...], w_ref[...], preferred_element_type=F32).astype(o_ref.dtype)


def _norm_matmul(x, g, w, layer, tm, tn):
    n, d = x.shape
    m = w.shape[2]
    return pl.pallas_call(
        _norm_mm_kernel,
        grid=(n // tm, m // tn),
        in_specs=[pl.BlockSpec((tm, d), lambda i, j: (i, 0)),
                  pl.BlockSpec((1, d), lambda i, j: (0, 0)),
                  pl.BlockSpec((None, d, tn), lambda i, j: (layer, 0, j))],
        out_specs=pl.BlockSpec((tm, tn), lambda i, j: (i, j)),
        out_shape=jax.ShapeDtypeStruct((n, m), BF16),
        scratch_shapes=[pltpu.VMEM((tm, d), BF16)],
        compiler_params=_cparams(("parallel", "arbitrary")),
        name="norm_matmul",
    )(x, g.reshape(1, d), w)


def _glu_kernel(h_ref, w1_ref, w3_ref, o_ref):
    a = _dot_cast_w(h_ref, w1_ref)
    b = _dot_cast_w(h_ref, w3_ref)
    o_ref[...] = (a * (1.0 / (1.0 + jnp.exp(-a))) * b).astype(o_ref.dtype)


def _glu(h, w1, w3, layer, tm, tn):
    n, d = h.shape
    m = w1.shape[2]
    return pl.pallas_call(
        _glu_kernel,
        grid=(n // tm, m // tn),
        in_specs=[pl.BlockSpec((tm, d), lambda i, j: (i, 0)),
                  pl.BlockSpec((None, d, tn), lambda i, j: (layer, 0, j)),
                  pl.BlockSpec((None, d, tn), lambda i, j: (layer, 0, j))],
        out_specs=pl.BlockSpec((tm, tn), lambda i, j: (i, j)),
        out_shape=jax.ShapeDtypeStruct((n, m), BF16),
        compiler_params=_cparams(("parallel", "arbitrary")),
        name="glu",
    )(h, w1, w3)


def _out_norm_kernel(a_ref, w_ref, r_ref, g_ref, x_ref, h_ref):
    x_ref[...] = r_ref[...] + jnp.dot(a_ref[...], w_ref[...], preferred_element_type=F32)
    _rmsnorm_rows(h_ref, x_ref, g_ref, 128)


def _out_proj_norm(a, w, res, g, tm):
    n, k = a.shape
    d = w.shape[1]
    return pl.pallas_call(
        _out_norm_kernel,
        grid=(n // tm,),
        in_specs=[pl.BlockSpec((tm, k), lambda i: (i, 0)),
                  pl.BlockSpec((k, d), lambda i: (0, 0)),
                  pl.BlockSpec((tm, d), lambda i: (i, 0)),
                  pl.BlockSpec((1, d), lambda i: (0, 0))],
        out_specs=[pl.BlockSpec((tm, d), lambda i: (i, 0)),
                   pl.BlockSpec((tm, d), lambda i: (i, 0))],
        out_shape=[jax.ShapeDtypeStruct((n, d), F32), jax.ShapeDtypeStruct((n, d), BF16)],
        compiler_params=_cparams(("parallel",)),
        name="out_proj_norm",
    )(a, w, res, g.reshape(1, d))


def _mm_res_kernel(a_ref, w_ref, r_ref, o_ref):
    o_ref[...] = r_ref[...] + jnp.dot(a_ref[...], w_ref[...], preferred_element_type=F32)


def _matmul_residual(a, w, layer, res, tm, tn):
    n, k = a.shape
    m = w.shape[2]
    return pl.pallas_call(
        _mm_res_kernel,
        grid=(n // tm, m // tn),
        in_specs=[pl.BlockSpec((tm, k), lambda i, j: (i, 0)),
                  pl.BlockSpec((None, k, tn), lambda i, j: (layer, 0, j)),
                  pl.BlockSpec((tm, tn), lambda i, j: (i, j))],
        out_specs=pl.BlockSpec((tm, tn), lambda i, j: (i, j)),
        out_shape=jax.ShapeDtypeStruct((n, m), F32),
        compiler_params=_cparams(("parallel", "arbitrary")),
        name="matmul_residual",
    )(a, w, res)


def _final_norm_kernel(x_ref, g_ref, o_ref):
    _rmsnorm_rows(o_ref, x_ref, g_ref, 128)


def _final_norm(x, g, tm):
    n, d = x.shape
    return pl.pallas_call(
        _final_norm_kernel,
        grid=(n // tm,),
        in_specs=[pl.BlockSpec((tm, d), lambda i: (i, 0)),
                  pl.BlockSpec((1, d), lambda i: (0, 0))],
        out_specs=pl.BlockSpec((tm, d), lambda i: (i, 0)),
        out_shape=jax.ShapeDtypeStruct((n, d), F32),
        compiler_params=_cparams(("parallel",)),
        name="final_norm",
    )(x, g.reshape(1, d))


def _split3_bf16(x):
    hi = x.astype(BF16)
    r1 = x - hi.astype(F32)
    mid = r1.astype(BF16)
    lo = (r1 - mid.astype(F32)).astype(BF16)
    return hi, mid, lo


def _gla_kernel(q_ref, k_ref, v_ref, g_ref, a_ref, wa2_ref, ba_ref, gn_ref, o_ref, st_ref,
                *, heads, dk, dv):
    tb = q_ref.shape[0]

    @pl.when(pl.program_id(1) == 0)
    def _():
        st_ref[...] = jnp.zeros_like(st_ref)

    z = jnp.dot(a_ref[...].astype(BF16), wa2_ref[...], preferred_element_type=F32) + ba_ref[...]
    la = (jnp.minimum(z, 0.0) - jnp.log1p(jnp.exp(-jnp.abs(z)))) * (1.0 / GLA_GATE_TAU)

    ri = lax.broadcasted_iota(jnp.int32, (CHUNK, CHUNK), 0)
    ci = lax.broadcasted_iota(jnp.int32, (CHUNK, CHUNK), 1)
    tri = jnp.where(ri >= ci, 1.0, 0.0).astype(BF16)
    q_scale = dk ** -0.5

    for c in range(tb // CHUNK):
        rs = slice(c * CHUNK, (c + 1) * CHUNK)
        hi, mid, lo = _split3_bf16(la[rs, :])
        lcum = (jnp.dot(tri, hi, preferred_element_type=F32)
                + jnp.dot(tri, mid, preferred_element_type=F32)
                + jnp.dot(tri, lo, preferred_element_type=F32))
        ltot = lcum[CHUNK - 1:CHUNK, :]
        kd = (k_ref[rs, :] * jnp.exp(ltot - lcum)).astype(BF16)
        dec = jnp.exp(ltot)
        qc = (q_ref[rs, :] * q_scale).astype(BF16)
        for h in range(heads):
            ks = slice(h * dk, (h + 1) * dk)
            vs = slice(h * dv, (h + 1) * dv)
            vt = v_ref[rs, vs].T.astype(BF16)
            s_new = st_ref[h] * dec[:, ks] + jnp.dot(vt, kd[:, ks], preferred_element_type=F32)
            st_ref[h] = s_new
            oh = lax.dot_general(qc[:, ks], s_new.astype(BF16), _NT,
                                 preferred_element_type=F32)
            ms = jnp.mean(oh * oh, axis=-1, keepdims=True)
            y = oh * lax.rsqrt(ms + RMS_EPS) * gn_ref[...]
            gate = g_ref[rs, vs].astype(F32)
            y = y * (gate * (1.0 / (1.0 + jnp.exp(-gate))))
            o_ref[rs, vs] = y.astype(o_ref.dtype)


def _gla_scan(proj, wa2p, ba, gn, batch, seq, d_qk, d_v, tb):
    n = proj.shape[0]
    nt = seq // tb
    heads = GLA_HEADS
    dk, dv = d_qk // heads, d_v // heads
    row = lambda b, t: b * nt + t
    kern = functools.partial(_gla_kernel, heads=heads, dk=dk, dv=dv)
    return pl.pallas_call(
        kern,
        grid=(batch, nt),
        in_specs=[pl.BlockSpec((tb, d_qk), lambda b, t: (row(b, t), 0)),
                  pl.BlockSpec((tb, d_qk), lambda b, t: (row(b, t), 1)),
                  pl.BlockSpec((tb, d_v), lambda b, t: (row(b, t), (2 * d_qk) // d_v)),
                  pl.BlockSpec((tb, d_v), lambda b, t: (row(b, t), (2 * d_qk + d_v) // d_v)),
                  pl.BlockSpec((tb, LANE), lambda b, t: (row(b, t), (2 * d_qk + 2 * d_v) // LANE)),
                  pl.BlockSpec((LANE, d_qk), lambda b, t: (0, 0)),
                  pl.BlockSpec((1, d_qk), lambda b, t: (0, 0)),
                  pl.BlockSpec((1, dv), lambda b, t: (0, 0))],
        out_specs=pl.BlockSpec((tb, d_v), lambda b, t: (row(b, t), 0)),
        out_shape=jax.ShapeDtypeStruct((n, d_v), BF16),
        scratch_shapes=[pltpu.VMEM((heads, dv, dk), F32)],
        compiler_params=_cparams(("arbitrary", "arbitrary")),
        name="gla_scan",
    )(proj, proj, proj, proj, proj, wa2p, ba.reshape(1, d_qk), gn.reshape(1, dv))


def _t5_bucket(rel):
    nb = REL_BUCKETS // 2
    max_exact = nb // 2
    ret = (rel > 0).astype(jnp.int32) * nb
    n = jnp.abs(rel)
    large = max_exact + (jnp.log(jnp.maximum(n, 1).astype(jnp.float32) / max_exact)
                         / math.log(REL_MAX_DIST / max_exact) * (nb - max_exact)).astype(jnp.int32)
    large = jnp.minimum(large, nb - 1)
    return ret + jnp.where(n < max_exact, n, large)


def _fold_rows(x, op):
    parts = [x[r * SUBLANE:(r + 1) * SUBLANE, :] for r in range(x.shape[0] // SUBLANE)]
    while len(parts) > 1:
        parts = [op(parts[i], parts[i + 1]) for i in range(0, len(parts), 2)]
    return parts[0]


def _dsa_kernel(*refs, topk):
    npair = IDX_HEADS // 2
    q_ref, c_ref, ki_ref, wi_ref = refs[:4]
    qip_refs = refs[4:4 + npair]
    (kvn_ref, kin_ref, wuvt_ref, bktt_ref, rb_ref, o_ref,
     crow_ref, ccol_ref, kirow_ref, bias_ref, scf_ref, scn_ref, lg0_ref,
     m_ref, l_ref, acc_ref) = refs[4 + npair:]
    b = pl.program_id(0)
    j = pl.program_id(1)
    H, QB, LAT = DSA_HEADS, Q_BLOCK, DSA_LATENT
    PAIR = 2 * QB
    far_bucket = REL_BUCKETS // 2 - 1
    n_far_max = scf_ref.shape[0]

    @pl.when((b == 0) & (j == 0))
    def _init():
        crow_ref[...] = jnp.zeros_like(crow_ref)
        kirow_ref[...] = jnp.zeros_like(kirow_ref)
        ccol_ref[:, 0:LAT, :] = jnp.zeros((ccol_ref.shape[0], LAT, QB), BF16)
        ccol_ref[:, LAT:, :] = jnp.ones((ccol_ref.shape[0], ONES_ROWS, QB), BF16)
        bk = bktt_ref[...]

        def head_bias(h, c):
            far = rb_ref[far_bucket, h]
            t = jnp.zeros(bk.shape, F32)
            for bb in range(REL_BUCKETS):
                t = jnp.where(bk == bb, (rb_ref[bb, h] - far) * LOG2E, t)
            bias_ref[h] = t
            return c
        lax.fori_loop(0, H, head_bias, 0)

    r_own = pl.multiple_of((j + 1) * QB, QB)
    c = c_ref[...].astype(F32)
    cn = c * lax.rsqrt(jnp.mean(c * c, axis=-1, keepdims=True) + RMS_EPS) * kvn_ref[...]
    crow_ref[pl.ds(r_own, QB), :] = (cn * (LAT ** -0.5 * LOG2E)).astype(BF16)
    ccol_ref[j + 1, 0:LAT, :] = cn.T.astype(BF16)
    ki = ki_ref[...].astype(F32)
    kin = ki * lax.rsqrt(jnp.mean(ki * ki, axis=-1, keepdims=True) + RMS_EPS) * kin_ref[...]
    kirow_ref[pl.ds(r_own, QB), :] = kin.astype(BF16)

    def q_group(p):
        return jnp.concatenate([q_ref[:, (ATT_GROUP * p + u) * LAT:(ATT_GROUP * p + u + 1) * LAT]
                                for u in range(ATT_GROUP)], axis=0)

    def qi_pair(p):
        return jnp.concatenate([qip_refs[p][:, u * IDX_DIM:(u + 1) * IDX_DIM] for u in range(2)], axis=0)

    wit = (wi_ref[...].astype(F32) * (IDX_HEADS ** -0.5 * IDX_DIM ** -0.5)).T

    nf = lax.shift_right_logical(j + 2, 2)
    tq_chunk = lax.shift_right_logical(lax.broadcasted_iota(jnp.int32, (1, QB), 1), int(math.log2(CHUNK)))
    rowf = lax.broadcasted_iota(jnp.int32, (FAR_W, QB), 0)
    rown = lax.broadcasted_iota(jnp.int32, (NEAR_W, QB), 0)
    adm_n = (rown < CHUNK * (QB // CHUNK + 1 + tq_chunk)) & ((j >= 1) | (rown >= QB))

    def score_of(kib):
        s = None
        sc_next = lax.dot_general(kib, qi_pair(0), _NT, preferred_element_type=F32)
        for p in range(H // 2):
            sc = sc_next
            if p + 1 < H // 2:
                sc_next = lax.dot_general(kib, qi_pair(p + 1), _NT, preferred_element_type=F32)
            for u in range(2):
                h = 2 * p + u
                t = jnp.maximum(sc[:, u * QB:(u + 1) * QB], 0.0) * wit[h:h + 1, :]
                s = t if s is None else s + t
        return s

    def far_rows(mb):
        return pl.ds(pl.multiple_of(QB + mb * FAR_W, QB), FAR_W)

    def far_score(mb, carry):
        mn, mx = carry
        s = score_of(kirow_ref[far_rows(mb), :])
        adm = rowf < (j - 1) * QB - mb * FAR_W
        scf_ref[mb] = jnp.where(adm, s, -jnp.inf)
        mx = jnp.maximum(mx, _fold_rows(jnp.where(adm, s, -jnp.inf), jnp.maximum))
        mn = jnp.minimum(mn, _fold_rows(jnp.where(adm, s, jnp.inf), jnp.minimum))
        return mn, mx

    mn, mx = lax.fori_loop(0, nf, far_score,
                           (jnp.full((SUBLANE, QB), jnp.inf, F32), jnp.full((SUBLANE, QB), -jnp.inf, F32)))
    r_near = pl.multiple_of(j * QB, QB)
    s = score_of(kirow_ref[pl.ds(r_near, NEAR_W), :])
    scn_ref[...] = jnp.where(adm_n, s, -jnp.inf)
    mx = jnp.maximum(mx, _fold_rows(jnp.where(adm_n, s, -jnp.inf), jnp.maximum))
    mn = jnp.minimum(mn, _fold_rows(jnp.where(adm_n, s, jnp.inf), jnp.minimum))
    mx = jnp.max(mx, axis=0, keepdims=True)
    mn = jnp.min(mn, axis=0, keepdims=True)

    kf = float(topk)

    def count_ge(mid):
        def fb(mb, acc):
            return acc + _fold_rows(jnp.where(scf_ref[mb] >= mid, 1.0, 0.0), jnp.add)
        acc = lax.fori_loop(0, nf, fb, jnp.zeros((SUBLANE, QB), F32))
        acc = acc + _fold_rows(jnp.where(scn_ref[...] >= mid, 1.0, 0.0), jnp.add)
        return jnp.sum(acc, axis=0, keepdims=True)

    def bis_step(lo, hi, cnt):
        mid = 0.5 * lo + 0.5 * hi
        cm = count_ge(mid)
        take = cm >= kf
        stuck = (mid <= lo) | (mid >= hi)
        lo2 = jnp.where(take, mid, lo)
        hi2 = jnp.where(take, hi, mid)
        cnt2 = jnp.where(take, cm, cnt)
        return lo2, hi2, cnt2, (cnt2 <= kf) | stuck

    def bis_cond(carry):
        it, _, _, _, pending = carry
        return (pending > 0.0) & (it < MAX_BISECT)

    def bis_body(carry):
        it, lo, hi, cnt, _ = carry
        done = None
        for _ in range(BISECT_UNROLL):
            lo, hi, cnt, done = bis_step(lo, hi, cnt)
        return it + BISECT_UNROLL, lo, hi, cnt, jnp.max(jnp.where(done, 0.0, 1.0))

    n_adm = ((2 * j + tq_chunk + 1) * CHUNK).astype(F32)
    pending0 = jnp.max(jnp.where(n_adm <= kf, 0.0, 1.0))
    _, lo, _, _, _ = lax.while_loop(bis_cond, bis_body, (jnp.int32(0), mn, mx, n_adm, pending0))

    m_ref[...] = jnp.full(m_ref.shape, MASK_NEG, F32)
    l_ref[...] = jnp.zeros_like(l_ref)
    acc_ref[...] = jnp.zeros_like(acc_ref)

    def logits(crow, p):
        return lax.dot_general(crow, q_group(p), _NT, preferred_element_type=F32)

    def attend(crow, ccol, madd, near, lg_first=None):
        gw = ATT_GROUP * QB
        ngroup = H // ATT_GROUP
        lg_next = lg_first if lg_first is not None else logits(crow, 0)
        pending = None

        def accumulate(ps, prob, alpha):
            pv = jnp.dot(ccol, prob, preferred_element_type=F32)
            acc_ref[:, ps] = acc_ref[:, ps] * alpha + pv[0:LAT, :]
            l_ref[:, ps] = l_ref[:, ps] * alpha + pv[LAT:LAT + 1, :]

        for p in range(ngroup):
            ps = slice(p * gw, (p + 1) * gw)
            lg = lg_next
            if p + 1 < ngroup:
                lg_next = logits(crow, p + 1)
            probs, alphas = [], []
            for u in range(ATT_GROUP):
                h = ATT_GROUP * p + u
                hs = slice(h * QB, (h + 1) * QB)
                sh = lg[:, u * QB:(u + 1) * QB] + madd
                if near:
                    sh = sh + bias_ref[h]
                m_old = m_ref[:, hs]
                m_new = jnp.maximum(m_old, jnp.max(sh, axis=0, keepdims=True))
                m_ref[:, hs] = m_new
                probs.append(jnp.exp2(sh - m_new).astype(BF16))
                alphas.append(jnp.exp2(m_old - m_new))
            if pending is not None:
                accumulate(*pending)
            pending = (ps, jnp.concatenate(probs, axis=1), jnp.concatenate(alphas, axis=1))
        accumulate(*pending)

    lg0_ref[...] = logits(crow_ref[far_rows(0), :], 0)

    def far_attend(mb, carry):
        madd = jnp.where(scf_ref[mb] >= lo, 0.0, MASK_NEG)
        blk = 1 + mb * (FAR_W // QB)
        ccol = jnp.concatenate([ccol_ref[blk + i] for i in range(FAR_W // QB)], axis=1)
        attend(crow_ref[far_rows(mb), :], ccol, madd, False, lg_first=lg0_ref[...])
        lg0_ref[...] = logits(crow_ref[far_rows(jnp.minimum(mb + 1, n_far_max - 1)), :], 0)
        return carry

    lax.fori_loop(0, nf, far_attend, 0)
    madd = jnp.where(scn_ref[...] >= lo, 0.0, MASK_NEG)
    ccol = jnp.concatenate([ccol_ref[j], ccol_ref[j + 1]], axis=1)
    attend(crow_ref[pl.ds(r_near, NEAR_W), :], ccol, madd, True)

    dvh = wuvt_ref.shape[1]
    for h in range(H):
        hs = slice(h * QB, (h + 1) * QB)
        oht = (acc_ref[:, hs] * (1.0 / l_ref[:, hs])).astype(BF16)
        out_t = jnp.dot(wuvt_ref[h], oht, preferred_element_type=F32)
        o_ref[:, h * dvh:(h + 1) * dvh] = out_t.T.astype(o_ref.dtype)


def _dsa_attention(proj, kvn, kin, wuvt, bktt, rel_bias, batch, seq, d_out):
    n = proj.shape[0]
    nq = seq // Q_BLOCK
    H, QB, LAT = DSA_HEADS, Q_BLOCK, DSA_LATENT
    topk = min(TOPK_MAX, seq // 4)
    row = lambda b, j: b * nq + j
    o_c = H * LAT
    o_qi = o_c + LAT
    o_ki = o_qi + IDX_HEADS * IDX_DIM
    o_wi = o_ki + IDX_DIM
    pair_w = 2 * IDX_DIM
    n_far = max(1, (nq + 1) // 4)
    const = lambda b, j: (0, 0)
    return pl.pallas_call(
        functools.partial(_dsa_kernel, topk=topk),
        grid=(batch, nq),
        in_specs=[pl.BlockSpec((QB, H * LAT), lambda b, j: (row(b, j), 0)),
                  pl.BlockSpec((QB, LAT), lambda b, j: (row(b, j), o_c // LAT)),
                  pl.BlockSpec((QB, IDX_DIM), lambda b, j: (row(b, j), o_ki // IDX_DIM)),
                  pl.BlockSpec((QB, LANE), lambda b, j: (row(b, j), o_wi // LANE))]
                 + [pl.BlockSpec((QB, pair_w), functools.partial(
                     lambda b, j, p: (row(b, j), o_qi // pair_w + p), p=p)) for p in range(IDX_HEADS // 2)]
                 + [pl.BlockSpec((1, LAT), const),
                    pl.BlockSpec((1, IDX_DIM), const),
                    pl.BlockSpec(wuvt.shape, lambda b, j: (0, 0, 0)),
                    pl.BlockSpec((NEAR_W, QB), const),
                    pl.BlockSpec(memory_space=pltpu.SMEM)],
        out_specs=pl.BlockSpec((QB, d_out), lambda b, j: (row(b, j), 0)),
        out_shape=jax.ShapeDtypeStruct((n, d_out), BF16),
        scratch_shapes=[pltpu.VMEM((seq + QB, LAT), BF16),
                        pltpu.VMEM((nq + 1, LAT + ONES_ROWS, QB), BF16),
                        pltpu.VMEM((seq + QB, IDX_DIM), BF16),
                        pltpu.VMEM((H, NEAR_W, QB), F32),
                        pltpu.VMEM((n_far, FAR_W, QB), F32),
                        pltpu.VMEM((NEAR_W, QB), F32),
                        pltpu.VMEM((FAR_W, ATT_GROUP * QB), F32),
                        pltpu.VMEM((1, H * QB), F32),
                        pltpu.VMEM((1, H * QB), F32),
                        pltpu.VMEM((LAT, H * QB), F32)],
        compiler_params=_cparams(("arbitrary", "arbitrary")),
        name="dsa_attention",
    )(*([proj] * (4 + IDX_HEADS // 2)), kvn.reshape(1, LAT), kin.reshape(1, IDX_DIM), wuvt, bktt, rel_bias)


def kernel(x, norm_mix, norm_ffn, norm_final, gla_w_in, gla_w_a2, gla_b_a, gla_g_norm, gla_w_out,
           dsa_w_in, dsa_kv_norm, dsa_kidx_norm, dsa_w_uv, dsa_w_out, rel_bias, ffn_w1, ffn_w3, ffn_w2):
    batch, seq, d = x.shape
    depth = norm_mix.shape[0]
    n = batch * seq
    d_qk = gla_w_a2.shape[2]
    d_v = gla_w_out.shape[1]
    tn = 512

    wk = jnp.arange(NEAR_W, dtype=jnp.int32)[:, None]
    tq = jnp.arange(Q_BLOCK, dtype=jnp.int32)[None, :]
    bktt = _t5_bucket(wk - Q_BLOCK - tq)

    gla_win = _cast_pad_cols(gla_w_in, tn)
    dsa_win = _cast_pad_cols(dsa_w_in, tn)
    w2 = _cast_weights(ffn_w2, tn)
    gla_wout = gla_w_out.astype(BF16)
    dsa_wout = dsa_w_out.astype(BF16)
    wa2p = jnp.pad(gla_w_a2.astype(BF16), ((0, 0), (0, LANE - GLA_GATE_RANK), (0, 0)))
    wuvt = jnp.swapaxes(dsa_w_uv, 2, 3).astype(BF16)

    xf = x.reshape(n, d)
    for i in range(depth):
        jm = i // 2
        if i % 2 == 0:
            proj = _norm_matmul(xf, norm_mix[i], gla_win, jm, min(1024, n), tn)
            u = _gla_scan(proj, wa2p[jm], gla_b_a[jm], gla_g_norm[jm], batch, seq, d_qk, d_v, min(256, seq))
            xf, h = _out_proj_norm(u, gla_wout[jm], xf, norm_ffn[i], min(512, n))
        else:
            proj = _norm_matmul(xf, norm_mix[i], dsa_win, jm, min(1024, n), tn)
            u = _dsa_attention(proj, dsa_kv_norm[jm], dsa_kidx_norm[jm], wuvt[jm],
                               bktt, rel_bias, batch, seq, dsa_w_out.shape[1])
            xf, h = _out_proj_norm(u, dsa_wout[jm], xf, norm_ffn[i], min(512, n))
        act = _glu(h, ffn_w1, ffn_w3, i, min(2048, n), tn)
        xf = _matmul_residual(act, w2, i, xf, min(1024, n), tn)
    return _final_norm(xf, norm_final, min(512, n)).reshape(batch, seq, d)
```

```python
import functools
import math

import jax
import jax.numpy as jnp
from jax import lax
from jax.experimental import pallas as pl
from jax.experimental.pallas import tpu as pltpu

F32 = jnp.float32
BF16 = jnp.bfloat16

RMS_EPS = 1e-6
CHUNK = 64

GLA_HEADS = 4
GLA_GATE_RANK = 16
GLA_GATE_TAU = 16.0

DSA_HEADS = 16
DSA_LATENT = 256
IDX_HEADS = 16
IDX_DIM = 128
TOPK_MAX = 256
Q_BLOCK = 128

REL_BUCKETS = 32
REL_MAX_DIST = 128

LANE = 128
SUBLANE = 8
FAR_W = 512
NEAR_W = 2 * Q_BLOCK
MASK_NEG = -1e30
LOG2E = math.log2(math.e)
ATT_GROUP = 2
ONES_ROWS = 16
BISECT_UNROLL = 4
SNAP_POP = 4
MAX_BISECT = 200
GLA_PROJ_TN = 1280
DSA_PROJ_TN = 1664
VMEM_LIMIT = 56 * 1024 * 1024

_NT = (((1,), (1,)), ((), ()))


def _cparams(sem):
    return pltpu.CompilerParams(dimension_semantics=sem, vmem_limit_bytes=VMEM_LIMIT)


def _rmsnorm_rows(dst_ref, x_ref, g_ref, rows):
    def body(r, c):
        sl = pl.ds(pl.multiple_of(r * rows, rows), rows)
        x = x_ref[sl, :]
        ms = jnp.mean(x * x, axis=-1, keepdims=True)
        dst_ref[sl, :] = (x * lax.rsqrt(ms + RMS_EPS) * g_ref[...]).astype(dst_ref.dtype)
        return c
    lax.fori_loop(0, x_ref.shape[0] // rows, body, 0)


def _dot_cast_w(h_ref, w_ref, kc=256):
    acc = None
    for k0 in range(0, w_ref.shape[0], kc):
        part = jnp.dot(h_ref[:, k0:k0 + kc], w_ref[k0:k0 + kc, :].astype(BF16),
                       preferred_element_type=F32)
        acc = part if acc is None else acc + part
    return acc


def _cast_kernel(w_ref, o_ref):
    o_ref[...] = w_ref[...].astype(o_ref.dtype)


def _cast_weights(w, tn):
    nl, k, m = w.shape
    return pl.pallas_call(
        _cast_kernel,
        grid=(nl, m // tn),
        in_specs=[pl.BlockSpec((None, k, tn), lambda l, j: (l, 0, j))],
        out_specs=pl.BlockSpec((None, k, tn), lambda l, j: (l, 0, j)),
        out_shape=jax.ShapeDtypeStruct((nl, k, m), BF16),
        compiler_params=_cparams(("parallel", "parallel")),
        name="cast_weights",
    )(w)


def _cast_pad_cols(w, mult):
    return jnp.pad(w.astype(BF16), ((0, 0), (0, 0), (0, (-w.shape[2]) % mult)))


def _norm_mm_kernel(x_ref, g_ref, w_ref, o_ref, h_ref):
    @pl.when(pl.program_id(1) == 0)
    def _():
        _rmsnorm_rows(h_ref, x_ref, g_ref, 128)
    o_ref[...] = jnp.dot(h_ref[...], w_ref[...], preferred_element_type=F32).astype(o_ref.dtype)


def _norm_matmul(x, g, w, layer, tm, tn):
    n, d = x.shape
    m = w.shape[2]
    return pl.pallas_call(
        _norm_mm_kernel,
        grid=(n // tm, m // tn),
        in_specs=[pl.BlockSpec((tm, d), lambda i, j: (i, 0)),
                  pl.BlockSpec((1, d), lambda i, j: (0, 0)),
                  pl.BlockSpec((None, d, tn), lambda i, j: (layer, 0, j))],
        out_specs=pl.BlockSpec((tm, tn), lambda i, j: (i, j)),
        out_shape=jax.ShapeDtypeStruct((n, m), BF16),
        scratch_shapes=[pltpu.VMEM((tm, d), BF16)],
        compiler_params=_cparams(("parallel", "arbitrary")),
        name="norm_matmul",
    )(x, g.reshape(1, d), w)


def _glu_kernel(h_ref, w1_ref, w3_ref, o_ref):
    a = _dot_cast_w(h_ref, w1_ref)
    b = _dot_cast_w(h_ref, w3_ref)
    o_ref[...] = (a * (1.0 / (1.0 + jnp.exp(-a))) * b).astype(o_ref.dtype)


def _glu(h, w1, w3, layer, tm, tn):
    n, d = h.shape
    m = w1.shape[2]
    return pl.pallas_call(
        _glu_kernel,
        grid=(n // tm, m // tn),
        in_specs=[pl.BlockSpec((tm, d), lambda i, j: (i, 0)),
                  pl.BlockSpec((None, d, tn), lambda i, j: (layer, 0, j)),
                  pl.BlockSpec((None, d, tn), lambda i, j: (layer, 0, j))],
        out_specs=pl.BlockSpec((tm, tn), lambda i, j: (i, j)),
        out_shape=jax.ShapeDtypeStruct((n, m), BF16),
        compiler_params=_cparams(("parallel", "arbitrary")),
        name="glu",
    )(h, w1, w3)


def _out_norm_kernel(a_ref, w_ref, r_ref, g_ref, x_ref, h_ref):
    x_ref[...] = r_ref[...] + jnp.dot(a_ref[...], w_ref[...], preferred_element_type=F32)
    _rmsnorm_rows(h_ref, x_ref, g_ref, 128)


def _out_proj_norm(a, w, res, g, tm):
    n, k = a.shape
    d = w.shape[1]
    return pl.pallas_call(
        _out_norm_kernel,
        grid=(n // tm,),
        in_specs=[pl.BlockSpec((tm, k), lambda i: (i, 0)),
                  pl.BlockSpec((k, d), lambda i: (0, 0)),
                  pl.BlockSpec((tm, d), lambda i: (i, 0)),
                  pl.BlockSpec((1, d), lambda i: (0, 0))],
        out_specs=[pl.BlockSpec((tm, d), lambda i: (i, 0)),
                   pl.BlockSpec((tm, d), lambda i: (i, 0))],
        out_shape=[jax.ShapeDtypeStruct((n, d), F32), jax.ShapeDtypeStruct((n, d), BF16)],
        compiler_params=_cparams(("parallel",)),
        name="out_proj_norm",
    )(a, w, res, g.reshape(1, d))


def _mm_res_kernel(a_ref, w_ref, r_ref, o_ref):
    o_ref[...] = r_ref[...] + jnp.dot(a_ref[...], w_ref[...], preferred_element_type=F32)


def _matmul_residual(a, w, layer, res, tm, tn):
    n, k = a.shape
    m = w.shape[2]
    return pl.pallas_call(
        _mm_res_kernel,
        grid=(n // tm, m // tn),
        in_specs=[pl.BlockSpec((tm, k), lambda i, j: (i, 0)),
                  pl.BlockSpec((None, k, tn), lambda i, j: (layer, 0, j)),
                  pl.BlockSpec((tm, tn), lambda i, j: (i, j))],
        out_specs=pl.BlockSpec((tm, tn), lambda i, j: (i, j)),
        out_shape=jax.ShapeDtypeStruct((n, m), F32),
        compiler_params=_cparams(("parallel", "arbitrary")),
        name="matmul_residual",
    )(a, w, res)


def _final_norm_kernel(x_ref, g_ref, o_ref):
    _rmsnorm_rows(o_ref, x_ref, g_ref, 128)


def _final_norm(x, g, tm):
    n, d = x.shape
    return pl.pallas_call(
        _final_norm_kernel,
        grid=(n // tm,),
        in_specs=[pl.BlockSpec((tm, d), lambda i: (i, 0)),
                  pl.BlockSpec((1, d), lambda i: (0, 0))],
        out_specs=pl.BlockSpec((tm, d), lambda i: (i, 0)),
        out_shape=jax.ShapeDtypeStruct((n, d), F32),
        compiler_params=_cparams(("parallel",)),
        name="final_norm",
    )(x, g.reshape(1, d))


def _split3_bf16(x):
    hi = x.astype(BF16)
    r1 = x - hi.astype(F32)
    mid = r1.astype(BF16)
    lo = (r1 - mid.astype(F32)).astype(BF16)
    return hi, mid, lo


def _gla_kernel(q_ref, k_ref, v_ref, g_ref, a_ref, wa2_ref, ba_ref, gn_ref, o_ref, st_ref,
                *, heads, dk, dv):
    tb = q_ref.shape[0]

    @pl.when(pl.program_id(1) == 0)
    def _():
        st_ref[...] = jnp.zeros_like(st_ref)

    z = jnp.dot(a_ref[...].astype(BF16), wa2_ref[...], preferred_element_type=F32) + ba_ref[...]
    la = (jnp.minimum(z, 0.0) - jnp.log1p(jnp.exp(-jnp.abs(z)))) * (1.0 / GLA_GATE_TAU)

    ri = lax.broadcasted_iota(jnp.int32, (CHUNK, CHUNK), 0)
    ci = lax.broadcasted_iota(jnp.int32, (CHUNK, CHUNK), 1)
    tri = jnp.where(ri >= ci, 1.0, 0.0).astype(BF16)
    q_scale = dk ** -0.5

    for c in range(tb // CHUNK):
        rs = slice(c * CHUNK, (c + 1) * CHUNK)
        hi, mid, lo = _split3_bf16(la[rs, :])
        lcum = (jnp.dot(tri, hi, preferred_element_type=F32)
                + jnp.dot(tri, mid, preferred_element_type=F32)
                + jnp.dot(tri, lo, preferred_element_type=F32))
        ltot = lcum[CHUNK - 1:CHUNK, :]
        kd = (k_ref[rs, :] * jnp.exp(ltot - lcum)).astype(BF16)
        dec = jnp.exp(ltot)
        qc = (q_ref[rs, :] * q_scale).astype(BF16)
        for h in range(heads):
            ks = slice(h * dk, (h + 1) * dk)
            vs = slice(h * dv, (h + 1) * dv)
            vt = v_ref[rs, vs].T.astype(BF16)
            s_new = st_ref[h] * dec[:, ks] + jnp.dot(vt, kd[:, ks], preferred_element_type=F32)
            st_ref[h] = s_new
            oh = lax.dot_general(qc[:, ks], s_new.astype(BF16), _NT,
                                 preferred_element_type=F32)
            ms = jnp.mean(oh * oh, axis=-1, keepdims=True)
            y = oh * lax.rsqrt(ms + RMS_EPS) * gn_ref[...]
            gate = g_ref[rs, vs].astype(F32)
            y = y * (gate * (1.0 / (1.0 + jnp.exp(-gate))))
            o_ref[rs, vs] = y.astype(o_ref.dtype)


def _gla_scan(proj, wa2p, ba, gn, batch, seq, d_qk, d_v, tb):
    n = proj.shape[0]
    nt = seq // tb
    heads = GLA_HEADS
    dk, dv = d_qk // heads, d_v // heads
    row = lambda b, t: b * nt + t
    kern = functools.partial(_gla_kernel, heads=heads, dk=dk, dv=dv)
    return pl.pallas_call(
        kern,
        grid=(batch, nt),
        in_specs=[pl.BlockSpec((tb, d_qk), lambda b, t: (row(b, t), 0)),
                  pl.BlockSpec((tb, d_qk), lambda b, t: (row(b, t), 1)),
                  pl.BlockSpec((tb, d_v), lambda b, t: (row(b, t), (2 * d_qk) // d_v)),
                  pl.BlockSpec((tb, d_v), lambda b, t: (row(b, t), (2 * d_qk + d_v) // d_v)),
                  pl.BlockSpec((tb, LANE), lambda b, t: (row(b, t), (2 * d_qk + 2 * d_v) // LANE)),
                  pl.BlockSpec((LANE, d_qk), lambda b, t: (0, 0)),
                  pl.BlockSpec((1, d_qk), lambda b, t: (0, 0)),
                  pl.BlockSpec((1, dv), lambda b, t: (0, 0))],
        out_specs=pl.BlockSpec((tb, d_v), lambda b, t: (row(b, t), 0)),
        out_shape=jax.ShapeDtypeStruct((n, d_v), BF16),
        scratch_shapes=[pltpu.VMEM((heads, dv, dk), F32)],
        compiler_params=_cparams(("arbitrary", "arbitrary")),
        name="gla_scan",
    )(proj, proj, proj, proj, proj, wa2p, ba.reshape(1, d_qk), gn.reshape(1, dv))


def _t5_bucket(rel):
    nb = REL_BUCKETS // 2
    max_exact = nb // 2
    ret = (rel > 0).astype(jnp.int32) * nb
    n = jnp.abs(rel)
    large = max_exact + (jnp.log(jnp.maximum(n, 1).astype(jnp.float32) / max_exact)
                         / math.log(REL_MAX_DIST / max_exact) * (nb - max_exact)).astype(jnp.int32)
    large = jnp.minimum(large, nb - 1)
    return ret + jnp.where(n < max_exact, n, large)


def _fold_rows(x, op):
    parts = [x[r * SUBLANE:(r + 1) * SUBLANE, :] for r in range(x.shape[0] // SUBLANE)]
    while len(parts) > 1:
        parts = [op(parts[i], parts[i + 1]) for i in range(0, len(parts), 2)]
    return parts[0]


def _dsa_kernel(*refs, topk):
    npair = IDX_HEADS // 2
    q_ref, c_ref, ki_ref, wi_ref = refs[:4]
    qip_refs = refs[4:4 + npair]
    (kvn_ref, kin_ref, wuvt_ref, bktt_ref, rb_ref, o_ref,
     crow_ref, ccol_ref, kirow_ref, bias_ref, scf_ref, scn_ref, lg0_ref,
     m_ref, l_ref, acc_ref) = refs[4 + npair:]
    b = pl.program_id(0)
    j = pl.program_id(1)
    H, QB, LAT = DSA_HEADS, Q_BLOCK, DSA_LATENT
    PAIR = 2 * QB
    far_bucket = REL_BUCKETS // 2 - 1
    n_far_max = scf_ref.shape[0]

    @pl.when((b == 0) & (j == 0))
    def _init():
        crow_ref[...] = jnp.zeros_like(crow_ref)
        kirow_ref[...] = jnp.zeros_like(kirow_ref)
        ccol_ref[:, 0:LAT, :] = jnp.zeros((ccol_ref.shape[0], LAT, QB), BF16)
        ccol_ref[:, LAT:, :] = jnp.ones((ccol_ref.shape[0], ONES_ROWS, QB), BF16)
        bk = bktt_ref[...]

        def head_bias(h, c):
            far = rb_ref[far_bucket, h]
            t = jnp.zeros(bk.shape, F32)
            for bb in range(REL_BUCKETS):
                t = jnp.where(bk == bb, (rb_ref[bb, h] - far) * LOG2E, t)
            bias_ref[h] = t
            return c
        lax.fori_loop(0, H, head_bias, 0)

    r_own = pl.multiple_of((j + 1) * QB, QB)
    c = c_ref[...].astype(F32)
    cn = c * lax.rsqrt(jnp.mean(c * c, axis=-1, keepdims=True) + RMS_EPS) * kvn_ref[...]
    crow_ref[pl.ds(r_own, QB), :] = (cn * (LAT ** -0.5 * LOG2E)).astype(BF16)
    ccol_ref[j + 1, 0:LAT, :] = cn.T.astype(BF16)
    ki = ki_ref[...].astype(F32)
    kin = ki * lax.rsqrt(jnp.mean(ki * ki, axis=-1, keepdims=True) + RMS_EPS) * kin_ref[...]
    kirow_ref[pl.ds(r_own, QB), :] = kin.astype(BF16)

    def q_group(p):
        return jnp.concatenate([q_ref[:, (ATT_GROUP * p + u) * LAT:(ATT_GROUP * p + u + 1) * LAT]
                                for u in range(ATT_GROUP)], axis=0)

    def qi_pair(p):
        return jnp.concatenate([qip_refs[p][:, u * IDX_DIM:(u + 1) * IDX_DIM] for u in range(2)], axis=0)

    wit = (wi_ref[...].astype(F32) * (IDX_HEADS ** -0.5 * IDX_DIM ** -0.5)).T

    nf = lax.shift_right_logical(j + 2, 2)
    tq_chunk = lax.shift_right_logical(lax.broadcasted_iota(jnp.int32, (1, QB), 1), int(math.log2(CHUNK)))
    rowf = lax.broadcasted_iota(jnp.int32, (FAR_W, QB), 0)
    rown = lax.broadcasted_iota(jnp.int32, (NEAR_W, QB), 0)
    adm_n = (rown < CHUNK * (QB // CHUNK + 1 + tq_chunk)) & ((j >= 1) | (rown >= QB))

    def score_of(kib):
        s = None
        sc_next = lax.dot_general(kib, qi_pair(0), _NT, preferred_element_type=F32)
        for p in range(H // 2):
            sc = sc_next
            if p + 1 < H // 2:
                sc_next = lax.dot_general(kib, qi_pair(p + 1), _NT, preferred_element_type=F32)
            for u in range(2):
                h = 2 * p + u
                t = jnp.maximum(sc[:, u * QB:(u + 1) * QB], 0.0) * wit[h:h + 1, :]
                s = t if s is None else s + t
        return s

    def far_rows(mb):
        return pl.ds(pl.multiple_of(QB + mb * FAR_W, QB), FAR_W)

    def far_score(mb, carry):
        mn, mx = carry
        s = score_of(kirow_ref[far_rows(mb), :])
        adm = rowf < (j - 1) * QB - mb * FAR_W
        scf_ref[mb] = jnp.where(adm, s, -jnp.inf)
        mx = jnp.maximum(mx, _fold_rows(jnp.where(adm, s, -jnp.inf), jnp.maximum))
        mn = jnp.minimum(mn, _fold_rows(jnp.where(adm, s, jnp.inf), jnp.minimum))
        return mn, mx

    mn, mx = lax.fori_loop(0, nf, far_score,
                           (jnp.full((SUBLANE, QB), jnp.inf, F32), jnp.full((SUBLANE, QB), -jnp.inf, F32)))
    r_near = pl.multiple_of(j * QB, QB)
    s = score_of(kirow_ref[pl.ds(r_near, NEAR_W), :])
    scn_ref[...] = jnp.where(adm_n, s, -jnp.inf)
    mx = jnp.maximum(mx, _fold_rows(jnp.where(adm_n, s, -jnp.inf), jnp.maximum))
    mn = jnp.minimum(mn, _fold_rows(jnp.where(adm_n, s, jnp.inf), jnp.minimum))
    mx = jnp.max(mx, axis=0, keepdims=True)
    mn = jnp.min(mn, axis=0, keepdims=True)

    kf = float(topk)

    def fold_scores(fn, op, init):
        def fb(mb, acc):
            return op(acc, _fold_rows(fn(scf_ref[mb]), op))
        acc = lax.fori_loop(0, nf, fb, jnp.full((SUBLANE, QB), init, F32))
        return op(acc, _fold_rows(fn(scn_ref[...]), op))

    def count_ge(t):
        return jnp.sum(fold_scores(lambda v: jnp.where(v >= t, 1.0, 0.0), jnp.add, 0.0),
                       axis=0, keepdims=True)

    def max_below(t):
        return jnp.max(fold_scores(lambda v: jnp.where(v < t, v, -jnp.inf), jnp.maximum, -jnp.inf),
                       axis=0, keepdims=True)

    def bis_step(lo, hi, clo, chi):
        mid = 0.5 * lo + 0.5 * hi
        cm = count_ge(mid)
        take = cm >= kf
        stuck = (mid <= lo) | (mid >= hi)
        lo2, clo2 = jnp.where(take, mid, lo), jnp.where(take, cm, clo)
        hi2, chi2 = jnp.where(take, hi, mid), jnp.where(take, chi, cm)
        narrow = (clo2 <= kf) | (clo2 - chi2 <= SNAP_POP) | stuck
        return lo2, hi2, clo2, chi2, narrow, stuck

    def bis_cond(carry):
        return (carry[-1] > 0.0) & (carry[0] < MAX_BISECT)

    def bis_body(carry):
        it, lo, hi, clo, chi, _, _ = carry
        narrow = stuck = None
        for _ in range(BISECT_UNROLL):
            lo, hi, clo, chi, narrow, stuck = bis_step(lo, hi, clo, chi)
        return (it + BISECT_UNROLL, lo, hi, clo, chi, jnp.where(stuck, 1.0, 0.0),
                jnp.max(jnp.where(narrow, 0.0, 1.0)))

    n_adm = ((2 * j + tq_chunk + 1) * CHUNK).astype(F32)
    hi0 = mx + (jnp.abs(mx) * 2.0 ** -20 + 1e-30)
    zero = jnp.zeros((1, QB), F32)
    wide0 = jnp.max(jnp.where(n_adm <= kf, 0.0, 1.0))
    _, lo, hi, clo, chi, stuck, _ = lax.while_loop(
        bis_cond, bis_body, (jnp.int32(0), mn, hi0, n_adm, zero, zero, wide0))

    def snap_active(clo, stuck):
        return (clo > kf) & (stuck == 0.0)

    def snap_cond(carry):
        return (carry[-1] > 0.0) & (carry[0] < SNAP_POP)

    def snap_body(carry):
        it, lo, hi, clo, chi, _ = carry
        t = max_below(hi)
        c = chi + 1.0
        active = snap_active(clo, stuck)
        fin = active & (c >= kf)
        step = active & jnp.logical_not(fin)
        lo, clo = jnp.where(fin, t, lo), jnp.where(fin, c, clo)
        hi, chi = jnp.where(step, t, hi), jnp.where(step, c, chi)
        return it + 1, lo, hi, clo, chi, jnp.max(jnp.where(snap_active(clo, stuck), 1.0, 0.0))

    snap0 = jnp.max(jnp.where(snap_active(clo, stuck), 1.0, 0.0))
    _, lo, _, _, _, _ = lax.while_loop(snap_cond, snap_body, (jnp.int32(0), lo, hi, clo, chi, snap0))

    m_ref[...] = jnp.full(m_ref.shape, MASK_NEG, F32)
    l_ref[...] = jnp.zeros_like(l_ref)
    acc_ref[...] = jnp.zeros_like(acc_ref)

    def logits(crow, p):
        return lax.dot_general(crow, q_group(p), _NT, preferred_element_type=F32)

    def attend(crow, ccol, madd, near, lg_first=None):
        gw = ATT_GROUP * QB
        ngroup = H // ATT_GROUP
        lg_next = lg_first if lg_first is not None else logits(crow, 0)
        pending = None

        def accumulate(ps, prob, alpha):
            pv = jnp.dot(ccol, prob, preferred_element_type=F32)
            acc_ref[:, ps] = acc_ref[:, ps] * alpha + pv[0:LAT, :]
            l_ref[:, ps] = l_ref[:, ps] * alpha + pv[LAT:LAT + 1, :]

        for p in range(ngroup):
            ps = slice(p * gw, (p + 1) * gw)
            lg = lg_next
            if p + 1 < ngroup:
                lg_next = logits(crow, p + 1)
            probs, alphas = [], []
            for u in range(ATT_GROUP):
                h = ATT_GROUP * p + u
                hs = slice(h * QB, (h + 1) * QB)
                sh = lg[:, u * QB:(u + 1) * QB] + madd
                if near:
                    sh = sh + bias_ref[h]
                m_old = m_ref[:, hs]
                m_new = jnp.maximum(m_old, jnp.max(sh, axis=0, keepdims=True))
                m_ref[:, hs] = m_new
                probs.append(jnp.exp2(sh - m_new).astype(BF16))
                alphas.append(jnp.exp2(m_old - m_new))
            if pending is not None:
                accumulate(*pending)
            pending = (ps, jnp.concatenate(probs, axis=1), jnp.concatenate(alphas, axis=1))
        accumulate(*pending)

    lg0_ref[...] = logits(crow_ref[far_rows(0), :], 0)

    def far_attend(mb, carry):
        madd = jnp.where(scf_ref[mb] >= lo, 0.0, MASK_NEG)
        blk = 1 + mb * (FAR_W // QB)
        ccol = jnp.concatenate([ccol_ref[blk + i] for i in range(FAR_W // QB)], axis=1)
        attend(crow_ref[far_rows(mb), :], ccol, madd, False, lg_first=lg0_ref[...])
        lg0_ref[...] = logits(crow_ref[far_rows(jnp.minimum(mb + 1, n_far_max - 1)), :], 0)
        return carry

    lax.fori_loop(0, nf, far_attend, 0)
    madd = jnp.where(scn_ref[...] >= lo, 0.0, MASK_NEG)
    ccol = jnp.concatenate([ccol_ref[j], ccol_ref[j + 1]], axis=1)
    attend(crow_ref[pl.ds(r_near, NEAR_W), :], ccol, madd, True)

    dvh = wuvt_ref.shape[1]
    for h in range(H):
        hs = slice(h * QB, (h + 1) * QB)
        oht = (acc_ref[:, hs] * (1.0 / l_ref[:, hs])).astype(BF16)
        out_t = jnp.dot(wuvt_ref[h], oht, preferred_element_type=F32)
        o_ref[:, h * dvh:(h + 1) * dvh] = out_t.T.astype(o_ref.dtype)


def _dsa_attention(proj, kvn, kin, wuvt, bktt, rel_bias, batch, seq, d_out):
    n = proj.shape[0]
    nq = seq // Q_BLOCK
    H, QB, LAT = DSA_HEADS, Q_BLOCK, DSA_LATENT
    topk = min(TOPK_MAX, seq // 4)
    row = lambda b, j: b * nq + j
    o_c = H * LAT
    o_qi = o_c + LAT
    o_ki = o_qi + IDX_HEADS * IDX_DIM
    o_wi = o_ki + IDX_DIM
    pair_w = 2 * IDX_DIM
    n_far = max(1, (nq + 1) // 4)
    const = lambda b, j: (0, 0)
    return pl.pallas_call(
        functools.partial(_dsa_kernel, topk=topk),
        grid=(batch, nq),
        in_specs=[pl.BlockSpec((QB, H * LAT), lambda b, j: (row(b, j), 0)),
                  pl.BlockSpec((QB, LAT), lambda b, j: (row(b, j), o_c // LAT)),
                  pl.BlockSpec((QB, IDX_DIM), lambda b, j: (row(b, j), o_ki // IDX_DIM)),
                  pl.BlockSpec((QB, LANE), lambda b, j: (row(b, j), o_wi // LANE))]
                 + [pl.BlockSpec((QB, pair_w), functools.partial(
                     lambda b, j, p: (row(b, j), o_qi // pair_w + p), p=p)) for p in range(IDX_HEADS // 2)]
                 + [pl.BlockSpec((1, LAT), const),
                    pl.BlockSpec((1, IDX_DIM), const),
                    pl.BlockSpec(wuvt.shape, lambda b, j: (0, 0, 0)),
                    pl.BlockSpec((NEAR_W, QB), const),
                    pl.BlockSpec(memory_space=pltpu.SMEM)],
        out_specs=pl.BlockSpec((QB, d_out), lambda b, j: (row(b, j), 0)),
        out_shape=jax.ShapeDtypeStruct((n, d_out), BF16),
        scratch_shapes=[pltpu.VMEM((seq + QB, LAT), BF16),
                        pltpu.VMEM((nq + 1, LAT + ONES_ROWS, QB), BF16),
                        pltpu.VMEM((seq + QB, IDX_DIM), BF16),
                        pltpu.VMEM((H, NEAR_W, QB), F32),
                        pltpu.VMEM((n_far, FAR_W, QB), F32),
                        pltpu.VMEM((NEAR_W, QB), F32),
                        pltpu.VMEM((FAR_W, ATT_GROUP * QB), F32),
                        pltpu.VMEM((1, H * QB), F32),
                        pltpu.VMEM((1, H * QB), F32),
                        pltpu.VMEM((LAT, H * QB), F32)],
        compiler_params=_cparams(("arbitrary", "arbitrary")),
        name="dsa_attention",
    )(*([proj] * (4 + IDX_HEADS // 2)), kvn.reshape(1, LAT), kin.reshape(1, IDX_DIM), wuvt, bktt, rel_bias)


def kernel(x, norm_mix, norm_ffn, norm_final, gla_w_in, gla_w_a2, gla_b_a, gla_g_norm, gla_w_out,
           dsa_w_in, dsa_kv_norm, dsa_kidx_norm, dsa_w_uv, dsa_w_out, rel_bias, ffn_w1, ffn_w3, ffn_w2):
    batch, seq, d = x.shape
    depth = norm_mix.shape[0]
    n = batch * seq
    d_qk = gla_w_a2.shape[2]
    d_v = gla_w_out.shape[1]
    tn = 512

    wk = jnp.arange(NEAR_W, dtype=jnp.int32)[:, None]
    tq = jnp.arange(Q_BLOCK, dtype=jnp.int32)[None, :]
    bktt = _t5_bucket(wk - Q_BLOCK - tq)

    gla_win = _cast_pad_cols(gla_w_in, GLA_PROJ_TN)
    dsa_win = _cast_pad_cols(dsa_w_in, DSA_PROJ_TN)
    w2 = _cast_weights(ffn_w2, tn)
    gla_wout = gla_w_out.astype(BF16)
    dsa_wout = dsa_w_out.astype(BF16)
    wa2p = jnp.pad(gla_w_a2.astype(BF16), ((0, 0), (0, LANE - GLA_GATE_RANK), (0, 0)))
    wuvt = jnp.swapaxes(dsa_w_uv, 2, 3).astype(BF16)

    xf = x.reshape(n, d)
    for i in range(depth):
        jm = i // 2
        if i % 2 == 0:
            proj = _norm_matmul(xf, norm_mix[i], gla_win, jm, min(1024, n), GLA_PROJ_TN)
            u = _gla_scan(proj, wa2p[jm], gla_b_a[jm], gla_g_norm[jm], batch, seq, d_qk, d_v, min(256, seq))
            xf, h = _out_proj_norm(u, gla_wout[jm], xf, norm_ffn[i], min(512, n))
        else:
            proj = _norm_matmul(xf, norm_mix[i], dsa_win, jm, min(1024, n), DSA_PROJ_TN)
            u = _dsa_attention(proj, dsa_kv_norm[jm], dsa_kidx_norm[jm], wuvt[jm],
                               bktt, rel_bias, batch, seq, dsa_w_out.shape[1])
            xf, h = _out_proj_norm(u, dsa_wout[jm], xf, norm_ffn[i], min(512, n))
        act = _glu(h, ffn_w1, ffn_w3, i, min(2048, n), tn)
        xf = _matmul_residual(act, w2, i, xf, min(1024, n), tn)
    return _final_norm(xf, norm_final, min(512, n)).reshape(batch, seq, d)
```

```python
import functools
import math

import jax
import jax.numpy as jnp
from jax import lax
from jax.experimental import pallas as pl
from jax.experimental.pallas import tpu as pltpu

F32 = jnp.float32
BF16 = jnp.bfloat16

RMS_EPS = 1e-6
CHUNK = 64

GLA_HEADS = 4
GLA_GATE_RANK = 16
GLA_GATE_TAU = 16.0

DSA_HEADS = 16
DSA_LATENT = 256
IDX_HEADS = 16
IDX_DIM = 128
TOPK_MAX = 256
Q_BLOCK = 128

REL_BUCKETS = 32
REL_MAX_DIST = 128

MXU_N = 256
LANE = 128
SUBLANE = 8
FAR_W = 512
NEAR_W = 2 * Q_BLOCK
MASK_NEG = -1e30
LOG2E = math.log2(math.e)
ATT_GROUP = 2
ONES_ROWS = 16
BISECT_UNROLL = 7
SNAP_POP = 8
MAX_BISECT = 200
GLA_PROJ_TN = 1280
DSA_PROJ_TN = 1664
VMEM_LIMIT = 56 * 1024 * 1024

_NT = (((1,), (1,)), ((), ()))


def _cparams(sem):
    return pltpu.CompilerParams(dimension_semantics=sem, vmem_limit_bytes=VMEM_LIMIT)


def _rmsnorm_rows(dst_ref, x_ref, g_ref, rows):
    def body(r, c):
        sl = pl.ds(pl.multiple_of(r * rows, rows), rows)
        x = x_ref[sl, :]
        ms = jnp.mean(x * x, axis=-1, keepdims=True)
        dst_ref[sl, :] = (x * lax.rsqrt(ms + RMS_EPS) * g_ref[...]).astype(dst_ref.dtype)
        return c
    lax.fori_loop(0, x_ref.shape[0] // rows, body, 0)


def _dot_cast_w(h_ref, w_ref, cols, kc=256):
    acc = None
    for k0 in range(0, w_ref.shape[0], kc):
        part = jnp.dot(h_ref[:, k0:k0 + kc], w_ref[k0:k0 + kc, cols].astype(BF16),
                       preferred_element_type=F32)
        acc = part if acc is None else acc + part
    return acc


def _cast_kernel(w_ref, o_ref):
    o_ref[...] = w_ref[...].astype(o_ref.dtype)


def _cast_weights(w, tn):
    nl, k, m = w.shape
    return pl.pallas_call(
        _cast_kernel,
        grid=(nl, m // tn),
        in_specs=[pl.BlockSpec((None, k, tn), lambda l, j: (l, 0, j))],
        out_specs=pl.BlockSpec((None, k, tn), lambda l, j: (l, 0, j)),
        out_shape=jax.ShapeDtypeStruct((nl, k, m), BF16),
        compiler_params=_cparams(("parallel", "parallel")),
        name="cast_weights",
    )(w)


def _cast_pad_cols(w, mult):
    return jnp.pad(w.astype(BF16), ((0, 0), (0, 0), (0, (-w.shape[2]) % mult)))


def _norm_mm_kernel(x_ref, g_ref, w_ref, o_ref, h_ref):
    @pl.when(pl.program_id(1) == 0)
    def _():
        _rmsnorm_rows(h_ref, x_ref, g_ref, 128)
    o_ref[...] = jnp.dot(h_ref[...], w_ref[...], preferred_element_type=F32).astype(o_ref.dtype)


def _norm_matmul(x, g, w, layer, tm, tn):
    n, d = x.shape
    m = w.shape[2]
    return pl.pallas_call(
        _norm_mm_kernel,
        grid=(n // tm, m // tn),
        in_specs=[pl.BlockSpec((tm, d), lambda i, j: (i, 0)),
                  pl.BlockSpec((1, d), lambda i, j: (0, 0)),
                  pl.BlockSpec((None, d, tn), lambda i, j: (layer, 0, j))],
        out_specs=pl.BlockSpec((tm, tn), lambda i, j: (i, j)),
        out_shape=jax.ShapeDtypeStruct((n, m), BF16),
        scratch_shapes=[pltpu.VMEM((tm, d), BF16)],
        compiler_params=_cparams(("parallel", "arbitrary")),
        name="norm_matmul",
    )(x, g.reshape(1, d), w)


def _glu_kernel(h_ref, w1_ref, w3_ref, o_ref):
    for c0 in range(0, o_ref.shape[1], MXU_N):
        cs = slice(c0, c0 + MXU_N)
        a = _dot_cast_w(h_ref, w1_ref, cs)
        b = _dot_cast_w(h_ref, w3_ref, cs)
        o_ref[:, cs] = (a * (1.0 / (1.0 + jnp.exp(-a))) * b).astype(o_ref.dtype)


def _glu(h, w1, w3, layer, tm, tn):
    n, d = h.shape
    m = w1.shape[2]
    return pl.pallas_call(
        _glu_kernel,
        grid=(n // tm, m // tn),
        in_specs=[pl.BlockSpec((tm, d), lambda i, j: (i, 0)),
                  pl.BlockSpec((None, d, tn), lambda i, j: (layer, 0, j)),
                  pl.BlockSpec((None, d, tn), lambda i, j: (layer, 0, j))],
        out_specs=pl.BlockSpec((tm, tn), lambda i, j: (i, j)),
        out_shape=jax.ShapeDtypeStruct((n, m), BF16),
        compiler_params=_cparams(("parallel", "arbitrary")),
        name="glu",
    )(h, w1, w3)


def _out_norm_kernel(a_ref, w_ref, r_ref, g_ref, x_ref, h_ref):
    half = a_ref.shape[0] // 2
    for r0 in (0, half):
        rs = slice(r0, r0 + half)
        x = r_ref[rs, :] + jnp.dot(a_ref[rs, :], w_ref[...], preferred_element_type=F32)
        x_ref[rs, :] = x
        ms = jnp.mean(x * x, axis=-1, keepdims=True)
        h_ref[rs, :] = (x * lax.rsqrt(ms + RMS_EPS) * g_ref[...]).astype(h_ref.dtype)


def _out_proj_norm(a, w, res, g, tm):
    n, k = a.shape
    d = w.shape[1]
    return pl.pallas_call(
        _out_norm_kernel,
        grid=(n // tm,),
        in_specs=[pl.BlockSpec((tm, k), lambda i: (i, 0)),
                  pl.BlockSpec((k, d), lambda i: (0, 0)),
                  pl.BlockSpec((tm, d), lambda i: (i, 0)),
                  pl.BlockSpec((1, d), lambda i: (0, 0))],
        out_specs=[pl.BlockSpec((tm, d), lambda i: (i, 0)),
                   pl.BlockSpec((tm, d), lambda i: (i, 0))],
        out_shape=[jax.ShapeDtypeStruct((n, d), F32), jax.ShapeDtypeStruct((n, d), BF16)],
        compiler_params=_cparams(("parallel",)),
        name="out_proj_norm",
    )(a, w, res, g.reshape(1, d))


def _mm_res_kernel(a_ref, w_ref, r_ref, o_ref):
    for c0 in range(0, o_ref.shape[1], MXU_N):
        cs = slice(c0, c0 + MXU_N)
        o_ref[:, cs] = r_ref[:, cs] + jnp.dot(a_ref[...], w_ref[:, cs], preferred_element_type=F32)


def _matmul_residual(a, w, layer, res, tm, tn):
    n, k = a.shape
    m = w.shape[2]
    return pl.pallas_call(
        _mm_res_kernel,
        grid=(n // tm, m // tn),
        in_specs=[pl.BlockSpec((tm, k), lambda i, j: (i, 0)),
                  pl.BlockSpec((None, k, tn), lambda i, j: (layer, 0, j)),
                  pl.BlockSpec((tm, tn), lambda i, j: (i, j))],
        out_specs=pl.BlockSpec((tm, tn), lambda i, j: (i, j)),
        out_shape=jax.ShapeDtypeStruct((n, m), F32),
        compiler_params=_cparams(("parallel", "arbitrary")),
        name="matmul_residual",
    )(a, w, res)


def _final_norm_kernel(x_ref, g_ref, o_ref):
    _rmsnorm_rows(o_ref, x_ref, g_ref, 128)


def _final_norm(x, g, tm):
    n, d = x.shape
    return pl.pallas_call(
        _final_norm_kernel,
        grid=(n // tm,),
        in_specs=[pl.BlockSpec((tm, d), lambda i: (i, 0)),
                  pl.BlockSpec((1, d), lambda i: (0, 0))],
        out_specs=pl.BlockSpec((tm, d), lambda i: (i, 0)),
        out_shape=jax.ShapeDtypeStruct((n, d), F32),
        compiler_params=_cparams(("parallel",)),
        name="final_norm",
    )(x, g.reshape(1, d))


def _split3_bf16(x):
    hi = x.astype(BF16)
    r1 = x - hi.astype(F32)
    mid = r1.astype(BF16)
    lo = (r1 - mid.astype(F32)).astype(BF16)
    return hi, mid, lo


def _gla_kernel(q_ref, k_ref, v_ref, g_ref, a_ref, wa2_ref, ba_ref, gn_ref, o_ref, st_ref,
                *, heads, dk, dv):
    tb = q_ref.shape[0]

    @pl.when(pl.program_id(1) == 0)
    def _():
        st_ref[...] = jnp.zeros_like(st_ref)

    z = jnp.dot(a_ref[...].astype(BF16), wa2_ref[...], preferred_element_type=F32) + ba_ref[...]
    la = (jnp.minimum(z, 0.0) - jnp.log1p(jnp.exp(-jnp.abs(z)))) * (1.0 / GLA_GATE_TAU)

    ri = lax.broadcasted_iota(jnp.int32, (CHUNK, CHUNK), 0)
    ci = lax.broadcasted_iota(jnp.int32, (CHUNK, CHUNK), 1)
    tri = jnp.where(ri >= ci, 1.0, 0.0).astype(BF16)
    q_scale = dk ** -0.5

    for c in range(tb // CHUNK):
        rs = slice(c * CHUNK, (c + 1) * CHUNK)
        hi, mid, lo = _split3_bf16(la[rs, :])
        lcum = (jnp.dot(tri, hi, preferred_element_type=F32)
                + jnp.dot(tri, mid, preferred_element_type=F32)
                + jnp.dot(tri, lo, preferred_element_type=F32))
        ltot = lcum[CHUNK - 1:CHUNK, :]
        kd = (k_ref[rs, :] * jnp.exp(ltot - lcum)).astype(BF16)
        dec = jnp.exp(ltot)
        qc = (q_ref[rs, :] * q_scale).astype(BF16)
        for h in range(heads):
            ks = slice(h * dk, (h + 1) * dk)
            vs = slice(h * dv, (h + 1) * dv)
            vt = v_ref[rs, vs].T.astype(BF16)
            s_new = st_ref[h] * dec[:, ks] + jnp.dot(vt, kd[:, ks], preferred_element_type=F32)
            st_ref[h] = s_new
            oh = lax.dot_general(qc[:, ks], s_new.astype(BF16), _NT,
                                 preferred_element_type=F32)
            ms = jnp.mean(oh * oh, axis=-1, keepdims=True)
            y = oh * lax.rsqrt(ms + RMS_EPS) * gn_ref[...]
            gate = g_ref[rs, vs].astype(F32)
            y = y * (gate * (1.0 / (1.0 + jnp.exp(-gate))))
            o_ref[rs, vs] = y.astype(o_ref.dtype)


def _gla_scan(proj, wa2p, ba, gn, batch, seq, d_qk, d_v, tb):
    n = proj.shape[0]
    nt = seq // tb
    heads = GLA_HEADS
    dk, dv = d_qk // heads, d_v // heads
    row = lambda b, t: b * nt + t
    kern = functools.partial(_gla_kernel, heads=heads, dk=dk, dv=dv)
    return pl.pallas_call(
        kern,
        grid=(batch, nt),
        in_specs=[pl.BlockSpec((tb, d_qk), lambda b, t: (row(b, t), 0)),
                  pl.BlockSpec((tb, d_qk), lambda b, t: (row(b, t), 1)),
                  pl.BlockSpec((tb, d_v), lambda b, t: (row(b, t), (2 * d_qk) // d_v)),
                  pl.BlockSpec((tb, d_v), lambda b, t: (row(b, t), (2 * d_qk + d_v) // d_v)),
                  pl.BlockSpec((tb, LANE), lambda b, t: (row(b, t), (2 * d_qk + 2 * d_v) // LANE)),
                  pl.BlockSpec((LANE, d_qk), lambda b, t: (0, 0)),
                  pl.BlockSpec((1, d_qk), lambda b, t: (0, 0)),
                  pl.BlockSpec((1, dv), lambda b, t: (0, 0))],
        out_specs=pl.BlockSpec((tb, d_v), lambda b, t: (row(b, t), 0)),
        out_shape=jax.ShapeDtypeStruct((n, d_v), BF16),
        scratch_shapes=[pltpu.VMEM((heads, dv, dk), F32)],
        compiler_params=_cparams(("arbitrary", "arbitrary")),
        name="gla_scan",
    )(proj, proj, proj, proj, proj, wa2p, ba.reshape(1, d_qk), gn.reshape(1, dv))


def _t5_bucket(rel):
    nb = REL_BUCKETS // 2
    max_exact = nb // 2
    ret = (rel > 0).astype(jnp.int32) * nb
    n = jnp.abs(rel)
    large = max_exact + (jnp.log(jnp.maximum(n, 1).astype(jnp.float32) / max_exact)
                         / math.log(REL_MAX_DIST / max_exact) * (nb - max_exact)).astype(jnp.int32)
    large = jnp.minimum(large, nb - 1)
    return ret + jnp.where(n < max_exact, n, large)


def _fold_rows(x, op):
    parts = [x[r * SUBLANE:(r + 1) * SUBLANE, :] for r in range(x.shape[0] // SUBLANE)]
    while len(parts) > 1:
        parts = [op(parts[i], parts[i + 1]) for i in range(0, len(parts), 2)]
    return parts[0]


def _dsa_kernel(*refs, topk):
    npair = IDX_HEADS // 2
    q_ref, c_ref, ki_ref, wi_ref = refs[:4]
    qip_refs = refs[4:4 + npair]
    (kvn_ref, kin_ref, wuvt_ref, bktt_ref, rb_ref, o_ref,
     crow_ref, ccol_ref, kirow_ref, bias_ref, scf_ref, scn_ref, lg0_ref, pp_ref, pa_ref,
     m_ref, l_ref, acc_ref) = refs[4 + npair:]
    b = pl.program_id(0)
    j = pl.program_id(1)
    H, QB, LAT = DSA_HEADS, Q_BLOCK, DSA_LATENT
    PAIR = 2 * QB
    far_bucket = REL_BUCKETS // 2 - 1
    n_far_max = scf_ref.shape[0]

    @pl.when((b == 0) & (j == 0))
    def _init():
        crow_ref[...] = jnp.zeros_like(crow_ref)
        kirow_ref[...] = jnp.zeros_like(kirow_ref)
        ccol_ref[:, 0:LAT, :] = jnp.zeros((ccol_ref.shape[0], LAT, QB), BF16)
        ccol_ref[:, LAT:, :] = jnp.ones((ccol_ref.shape[0], ONES_ROWS, QB), BF16)
        bk = bktt_ref[...]

        def head_bias(h, c):
            far = rb_ref[far_bucket, h]
            t = jnp.zeros(bk.shape, F32)
            for bb in range(REL_BUCKETS):
                t = jnp.where(bk == bb, (rb_ref[bb, h] - far) * LOG2E, t)
            bias_ref[h] = t
            return c
        lax.fori_loop(0, H, head_bias, 0)

    r_own = pl.multiple_of((j + 1) * QB, QB)
    c = c_ref[...].astype(F32)
    cn = c * lax.rsqrt(jnp.mean(c * c, axis=-1, keepdims=True) + RMS_EPS) * kvn_ref[...]
    crow_ref[pl.ds(r_own, QB), :] = (cn * (LAT ** -0.5 * LOG2E)).astype(BF16)
    ccol_ref[j + 1, 0:LAT, :] = cn.T.astype(BF16)
    ki = ki_ref[...].astype(F32)
    kin = ki * lax.rsqrt(jnp.mean(ki * ki, axis=-1, keepdims=True) + RMS_EPS) * kin_ref[...]
    kirow_ref[pl.ds(r_own, QB), :] = kin.astype(BF16)

    def q_group(p):
        return jnp.concatenate([q_ref[:, (ATT_GROUP * p + u) * LAT:(ATT_GROUP * p + u + 1) * LAT]
                                for u in range(ATT_GROUP)], axis=0)

    def qi_pair(p):
        return jnp.concatenate([qip_refs[p][:, u * IDX_DIM:(u + 1) * IDX_DIM] for u in range(2)], axis=0)

    wit = (wi_ref[...].astype(F32) * (IDX_HEADS ** -0.5 * IDX_DIM ** -0.5)).T

    nf = lax.shift_right_logical(j + 2, 2)
    tq_chunk = lax.shift_right_logical(lax.broadcasted_iota(jnp.int32, (1, QB), 1), int(math.log2(CHUNK)))
    rowf = lax.broadcasted_iota(jnp.int32, (FAR_W, QB), 0)
    rown = lax.broadcasted_iota(jnp.int32, (NEAR_W, QB), 0)
    adm_n = (rown < CHUNK * (QB // CHUNK + 1 + tq_chunk)) & ((j >= 1) | (rown >= QB))

    def score_of(kib):
        s = None
        sc_next = lax.dot_general(kib, qi_pair(0), _NT, preferred_element_type=F32)
        for p in range(H // 2):
            sc = sc_next
            if p + 1 < H // 2:
                sc_next = lax.dot_general(kib, qi_pair(p + 1), _NT, preferred_element_type=F32)
            for u in range(2):
                h = 2 * p + u
                t = jnp.maximum(sc[:, u * QB:(u + 1) * QB], 0.0) * wit[h:h + 1, :]
                s = t if s is None else s + t
        return s

    def far_rows(mb):
        return pl.ds(pl.multiple_of(QB + mb * FAR_W, QB), FAR_W)

    def far_score(mb, carry):
        mn, mx = carry
        s = score_of(kirow_ref[far_rows(mb), :])
        adm = rowf < (j - 1) * QB - mb * FAR_W
        scf_ref[mb] = jnp.where(adm, s, -jnp.inf)
        mx = jnp.maximum(mx, _fold_rows(jnp.where(adm, s, -jnp.inf), jnp.maximum))
        mn = jnp.minimum(mn, _fold_rows(jnp.where(adm, s, jnp.inf), jnp.minimum))
        return mn, mx

    mn, mx = lax.fori_loop(0, nf, far_score,
                           (jnp.full((SUBLANE, QB), jnp.inf, F32), jnp.full((SUBLANE, QB), -jnp.inf, F32)))
    r_near = pl.multiple_of(j * QB, QB)
    s = score_of(kirow_ref[pl.ds(r_near, NEAR_W), :])
    scn_ref[...] = jnp.where(adm_n, s, -jnp.inf)
    mx = jnp.maximum(mx, _fold_rows(jnp.where(adm_n, s, -jnp.inf), jnp.maximum))
    mn = jnp.minimum(mn, _fold_rows(jnp.where(adm_n, s, jnp.inf), jnp.minimum))
    mx = jnp.max(mx, axis=0, keepdims=True)
    mn = jnp.min(mn, axis=0, keepdims=True)

    kf = float(topk)

    def fold_scores(fn, op, init):
        def fb(mb, acc):
            return op(acc, _fold_rows(fn(scf_ref[mb]), op))
        acc = lax.fori_loop(0, nf, fb, jnp.full((SUBLANE, QB), init, F32))
        return op(acc, _fold_rows(fn(scn_ref[...]), op))

    def count_ge(t):
        return jnp.sum(fold_scores(lambda v: jnp.where(v >= t, 1.0, 0.0), jnp.add, 0.0),
                       axis=0, keepdims=True)

    def max_below(t):
        return jnp.max(fold_scores(lambda v: jnp.where(v < t, v, -jnp.inf), jnp.maximum, -jnp.inf),
                       axis=0, keepdims=True)

    def bis_step(lo, hi, clo, chi):
        mid = 0.5 * lo + 0.5 * hi
        cm = count_ge(mid)
        take = cm >= kf
        stuck = (mid <= lo) | (mid >= hi)
        lo2, clo2 = jnp.where(take, mid, lo), jnp.where(take, cm, clo)
        hi2, chi2 = jnp.where(take, hi, mid), jnp.where(take, chi, cm)
        narrow = (clo2 <= kf) | (clo2 - chi2 <= SNAP_POP) | stuck
        return lo2, hi2, clo2, chi2, narrow, stuck

    def bis_cond(carry):
        return (carry[-1] > 0.0) & (carry[0] < MAX_BISECT)

    def bis_body(carry):
        it, lo, hi, clo, chi, _, _ = carry
        narrow = stuck = None
        for _ in range(BISECT_UNROLL):
            lo, hi, clo, chi, narrow, stuck = bis_step(lo, hi, clo, chi)
        return (it + BISECT_UNROLL, lo, hi, clo, chi, jnp.where(stuck, 1.0, 0.0),
                jnp.max(jnp.where(narrow, 0.0, 1.0)))

    n_adm = ((2 * j + tq_chunk + 1) * CHUNK).astype(F32)
    hi0 = mx + (jnp.abs(mx) * 2.0 ** -20 + 1e-30)
    zero = jnp.zeros((1, QB), F32)
    wide0 = jnp.max(jnp.where(n_adm <= kf, 0.0, 1.0))
    _, lo, hi, clo, chi, stuck, _ = lax.while_loop(
        bis_cond, bis_body, (jnp.int32(0), mn, hi0, n_adm, zero, zero, wide0))

    def snap_active(clo, stuck):
        return (clo > kf) & (stuck == 0.0)

    def snap_cond(carry):
        return (carry[-1] > 0.0) & (carry[0] < SNAP_POP)

    def snap_body(carry):
        it, lo, hi, clo, chi, _ = carry
        t = max_below(hi)
        c = chi + 1.0
        active = snap_active(clo, stuck)
        fin = active & (c >= kf)
        step = active & jnp.logical_not(fin)
        lo, clo = jnp.where(fin, t, lo), jnp.where(fin, c, clo)
        hi, chi = jnp.where(step, t, hi), jnp.where(step, c, chi)
        return it + 1, lo, hi, clo, chi, jnp.max(jnp.where(snap_active(clo, stuck), 1.0, 0.0))

    snap0 = jnp.max(jnp.where(snap_active(clo, stuck), 1.0, 0.0))
    _, lo, _, _, _, _ = lax.while_loop(snap_cond, snap_body, (jnp.int32(0), lo, hi, clo, chi, snap0))

    m_ref[...] = jnp.full(m_ref.shape, MASK_NEG, F32)
    l_ref[...] = jnp.zeros_like(l_ref)
    acc_ref[...] = jnp.zeros_like(acc_ref)

    def logits(crow, p):
        return lax.dot_general(crow, q_group(p), _NT, preferred_element_type=F32)

    gw = ATT_GROUP * QB
    ngroup = H // ATT_GROUP
    last_ps = slice((ngroup - 1) * gw, ngroup * gw)

    def accumulate(ps, prob, alpha, ccol):
        pv = jnp.dot(ccol, prob, preferred_element_type=F32)
        acc_ref[:, ps] = acc_ref[:, ps] * alpha + pv[0:LAT, :]
        l_ref[:, ps] = l_ref[:, ps] * alpha + pv[LAT:LAT + 1, :]

    def attend(crow, ccol, madd, near, lg_first, pending):
        lg_next = lg_first if lg_first is not None else logits(crow, 0)
        for p in range(ngroup):
            ps = slice(p * gw, (p + 1) * gw)
            lg = lg_next
            if p + 1 < ngroup:
                lg_next = logits(crow, p + 1)
            probs, alphas = [], []
            for u in range(ATT_GROUP):
                h = ATT_GROUP * p + u
                hs = slice(h * QB, (h + 1) * QB)
                sh = lg[:, u * QB:(u + 1) * QB] + madd
                if near:
                    sh = sh + bias_ref[h]
                m_old = m_ref[:, hs]
                m_new = jnp.maximum(m_old, jnp.max(sh, axis=0, keepdims=True))
                m_ref[:, hs] = m_new
                probs.append(jnp.exp2(sh - m_new).astype(BF16))
                alphas.append(jnp.exp2(m_old - m_new))
            accumulate(*pending)
            pending = (ps, jnp.concatenate(probs, axis=1), jnp.concatenate(alphas, axis=1), ccol)
        return pending

    def far_ccol(mb):
        blk = 1 + mb * (FAR_W // QB)
        return jnp.concatenate([ccol_ref[blk + i] for i in range(FAR_W // QB)], axis=1)

    lg0_ref[...] = logits(crow_ref[far_rows(0), :], 0)
    pp_ref[...] = jnp.zeros_like(pp_ref)
    pa_ref[...] = jnp.ones_like(pa_ref)

    def far_attend(mb, carry):
        madd = jnp.where(scf_ref[mb] >= lo, 0.0, MASK_NEG)
        carried = (last_ps, pp_ref[...], pa_ref[...], far_ccol(jnp.maximum(mb - 1, 0)))
        _, prob, alpha, _ = attend(crow_ref[far_rows(mb), :], far_ccol(mb), madd, False, lg0_ref[...], carried)
        pp_ref[...] = prob
        pa_ref[...] = alpha
        lg0_ref[...] = logits(crow_ref[far_rows(jnp.minimum(mb + 1, n_far_max - 1)), :], 0)
        return carry

    lax.fori_loop(0, nf, far_attend, 0)
    madd = jnp.where(scn_ref[...] >= lo, 0.0, MASK_NEG)
    carried = (last_ps, pp_ref[...], pa_ref[...], far_ccol(jnp.maximum(nf - 1, 0)))
    ccol = jnp.concatenate([ccol_ref[j], ccol_ref[j + 1]], axis=1)
    accumulate(*attend(crow_ref[pl.ds(r_near, NEAR_W), :], ccol, madd, True, None, carried))

    dvh = wuvt_ref.shape[1]
    for h in range(H):
        hs = slice(h * QB, (h + 1) * QB)
        oht = (acc_ref[:, hs] * (1.0 / l_ref[:, hs])).astype(BF16)
        out_t = jnp.dot(wuvt_ref[h], oht, preferred_element_type=F32)
        o_ref[:, h * dvh:(h + 1) * dvh] = out_t.T.astype(o_ref.dtype)


def _dsa_attention(proj, kvn, kin, wuvt, bktt, rel_bias, batch, seq, d_out):
    n = proj.shape[0]
    nq = seq // Q_BLOCK
    H, QB, LAT = DSA_HEADS, Q_BLOCK, DSA_LATENT
    topk = min(TOPK_MAX, seq // 4)
    row = lambda b, j: b * nq + j
    o_c = H * LAT
    o_qi = o_c + LAT
    o_ki = o_qi + IDX_HEADS * IDX_DIM
    o_wi = o_ki + IDX_DIM
    pair_w = 2 * IDX_DIM
    n_far = max(1, (nq + 1) // 4)
    const = lambda b, j: (0, 0)
    return pl.pallas_call(
        functools.partial(_dsa_kernel, topk=topk),
        grid=(batch, nq),
        in_specs=[pl.BlockSpec((QB, H * LAT), lambda b, j: (row(b, j), 0)),
                  pl.BlockSpec((QB, LAT), lambda b, j: (row(b, j), o_c // LAT)),
                  pl.BlockSpec((QB, IDX_DIM), lambda b, j: (row(b, j), o_ki // IDX_DIM)),
                  pl.BlockSpec((QB, LANE), lambda b, j: (row(b, j), o_wi // LANE))]
                 + [pl.BlockSpec((QB, pair_w), functools.partial(
                     lambda b, j, p: (row(b, j), o_qi // pair_w + p), p=p)) for p in range(IDX_HEADS // 2)]
                 + [pl.BlockSpec((1, LAT), const),
                    pl.BlockSpec((1, IDX_DIM), const),
                    pl.BlockSpec(wuvt.shape, lambda b, j: (0, 0, 0)),
                    pl.BlockSpec((NEAR_W, QB), const),
                    pl.BlockSpec(memory_space=pltpu.SMEM)],
        out_specs=pl.BlockSpec((QB, d_out), lambda b, j: (row(b, j), 0)),
        out_shape=jax.ShapeDtypeStruct((n, d_out), BF16),
        scratch_shapes=[pltpu.VMEM((seq + QB, LAT), BF16),
                        pltpu.VMEM((nq + 1, LAT + ONES_ROWS, QB), BF16),
                        pltpu.VMEM((seq + QB, IDX_DIM), BF16),
                        pltpu.VMEM((H, NEAR_W, QB), F32),
                        pltpu.VMEM((n_far, FAR_W, QB), F32),
                        pltpu.VMEM((NEAR_W, QB), F32),
                        pltpu.VMEM((FAR_W, ATT_GROUP * QB), F32),
                        pltpu.VMEM((FAR_W, ATT_GROUP * QB), BF16),
                        pltpu.VMEM((1, ATT_GROUP * QB), F32),
                        pltpu.VMEM((1, H * QB), F32),
                        pltpu.VMEM((1, H * QB), F32),
                        pltpu.VMEM((LAT, H * QB), F32)],
        compiler_params=_cparams(("arbitrary", "arbitrary")),
        name="dsa_attention",
    )(*([proj] * (4 + IDX_HEADS // 2)), kvn.reshape(1, LAT), kin.reshape(1, IDX_DIM), wuvt, bktt, rel_bias)


def kernel(x, norm_mix, norm_ffn, norm_final, gla_w_in, gla_w_a2, gla_b_a, gla_g_norm, gla_w_out,
           dsa_w_in, dsa_kv_norm, dsa_kidx_norm, dsa_w_uv, dsa_w_out, rel_bias, ffn_w1, ffn_w3, ffn_w2):
    batch, seq, d = x.shape
    depth = norm_mix.shape[0]
    n = batch * seq
    d_qk = gla_w_a2.shape[2]
    d_v = gla_w_out.shape[1]
    tn = 512

    wk = jnp.arange(NEAR_W, dtype=jnp.int32)[:, None]
    tq = jnp.arange(Q_BLOCK, dtype=jnp.int32)[None, :]
    bktt = _t5_bucket(wk - Q_BLOCK - tq)

    gla_win = _cast_pad_cols(gla_w_in, GLA_PROJ_TN)
    dsa_win = _cast_pad_cols(dsa_w_in, DSA_PROJ_TN)
    w2 = _cast_weights(ffn_w2, tn)
    gla_wout = gla_w_out.astype(BF16)
    dsa_wout = dsa_w_out.astype(BF16)
    wa2p = jnp.pad(gla_w_a2.astype(BF16), ((0, 0), (0, LANE - GLA_GATE_RANK), (0, 0)))
    wuvt = jnp.swapaxes(dsa_w_uv, 2, 3).astype(BF16)

    xf = x.reshape(n, d)
    for i in range(depth):
        jm = i // 2
        if i % 2 == 0:
            proj = _norm_matmul(xf, norm_mix[i], gla_win, jm, min(1024, n), GLA_PROJ_TN)
            u = _gla_scan(proj, wa2p[jm], gla_b_a[jm], gla_g_norm[jm], batch, seq, d_qk, d_v, min(256, seq))
            xf, h = _out_proj_norm(u, gla_wout[jm], xf, norm_ffn[i], min(512, n))
        else:
            proj = _norm_matmul(xf, norm_mix[i], dsa_win, jm, min(1024, n), DSA_PROJ_TN)
            u = _dsa_attention(proj, dsa_kv_norm[jm], dsa_kidx_norm[jm], wuvt[jm],
                               bktt, rel_bias, batch, seq, dsa_w_out.shape[1])
            xf, h = _out_proj_norm(u, dsa_wout[jm], xf, norm_ffn[i], min(512, n))
        act = _glu(h, ffn_w1, ffn_w3, i, min(2048, n), tn)
        xf = _matmul_residual(act, w2, i, xf, min(1024, n), tn)
    return _final_norm(xf, norm_final, min(512, n)).reshape(batch, seq, d)
```

```python
import functools
import math

import jax
import jax.numpy as jnp
from jax import lax
from jax.experimental import pallas as pl
from jax.experimental.pallas import tpu as pltpu

F32 = jnp.float32
BF16 = jnp.bfloat16

RMS_EPS = 1e-6
CHUNK = 64

GLA_HEADS = 4
GLA_GATE_RANK = 16
GLA_GATE_TAU = 16.0

DSA_HEADS = 16
DSA_LATENT = 256
IDX_HEADS = 16
IDX_DIM = 128
TOPK_MAX = 256
Q_BLOCK = 128

REL_BUCKETS = 32
REL_MAX_DIST = 128

MXU_N = 256
LANE = 128
SUBLANE = 8
FAR_W = 512
NEAR_W = 2 * Q_BLOCK
MASK_NEG = -1e30
LOG2E = math.log2(math.e)
ATT_GROUP = 2
ONES_ROWS = 16
BISECT_UNROLL = 7
SNAP_POP = 8
MAX_BISECT = 200
GLA_PROJ_TN = 1280
DSA_PROJ_TN = 1664
VMEM_LIMIT = 56 * 1024 * 1024

_NT = (((1,), (1,)), ((), ()))


def _cparams(sem):
    return pltpu.CompilerParams(dimension_semantics=sem, vmem_limit_bytes=VMEM_LIMIT)


def _rmsnorm_rows(dst_ref, x_ref, g_ref, rows):
    def body(r, c):
        sl = pl.ds(pl.multiple_of(r * rows, rows), rows)
        x = x_ref[sl, :]
        ms = jnp.mean(x * x, axis=-1, keepdims=True)
        dst_ref[sl, :] = (x * lax.rsqrt(ms + RMS_EPS) * g_ref[...]).astype(dst_ref.dtype)
        return c
    lax.fori_loop(0, x_ref.shape[0] // rows, body, 0)


def _dot_cast_w(h_ref, w_ref, cols, kc=256):
    acc = None
    for k0 in range(0, w_ref.shape[0], kc):
        part = jnp.dot(h_ref[:, k0:k0 + kc], w_ref[k0:k0 + kc, cols].astype(BF16),
                       preferred_element_type=F32)
        acc = part if acc is None else acc + part
    return acc


def _cast_kernel(w_ref, o_ref):
    o_ref[...] = w_ref[...].astype(o_ref.dtype)


def _cast_weights(w, tn):
    nl, k, m = w.shape
    return pl.pallas_call(
        _cast_kernel,
        grid=(nl, m // tn),
        in_specs=[pl.BlockSpec((None, k, tn), lambda l, j: (l, 0, j))],
        out_specs=pl.BlockSpec((None, k, tn), lambda l, j: (l, 0, j)),
        out_shape=jax.ShapeDtypeStruct((nl, k, m), BF16),
        compiler_params=_cparams(("parallel", "parallel")),
        name="cast_weights",
    )(w)


def _cast_pad_cols(w, mult):
    return jnp.pad(w.astype(BF16), ((0, 0), (0, 0), (0, (-w.shape[2]) % mult)))


def _norm_mm_kernel(x_ref, g_ref, w_ref, o_ref, h_ref):
    @pl.when(pl.program_id(1) == 0)
    def _():
        _rmsnorm_rows(h_ref, x_ref, g_ref, 128)
    o_ref[...] = jnp.dot(h_ref[...], w_ref[...], preferred_element_type=F32).astype(o_ref.dtype)


def _norm_matmul(x, g, w, layer, tm, tn):
    n, d = x.shape
    m = w.shape[2]
    return pl.pallas_call(
        _norm_mm_kernel,
        grid=(n // tm, m // tn),
        in_specs=[pl.BlockSpec((tm, d), lambda i, j: (i, 0)),
                  pl.BlockSpec((1, d), lambda i, j: (0, 0)),
                  pl.BlockSpec((None, d, tn), lambda i, j: (layer, 0, j))],
        out_specs=pl.BlockSpec((tm, tn), lambda i, j: (i, j)),
        out_shape=jax.ShapeDtypeStruct((n, m), BF16),
        scratch_shapes=[pltpu.VMEM((tm, d), BF16)],
        compiler_params=_cparams(("parallel", "arbitrary")),
        name="norm_matmul",
    )(x, g.reshape(1, d), w)


def _glu_kernel(h_ref, w1_ref, w3_ref, o_ref):
    for c0 in range(0, o_ref.shape[1], MXU_N):
        cs = slice(c0, c0 + MXU_N)
        a = _dot_cast_w(h_ref, w1_ref, cs)
        b = _dot_cast_w(h_ref, w3_ref, cs)
        o_ref[:, cs] = (a * (1.0 / (1.0 + jnp.exp(-a))) * b).astype(o_ref.dtype)


def _glu(h, w1, w3, layer, tm, tn):
    n, d = h.shape
    m = w1.shape[2]
    return pl.pallas_call(
        _glu_kernel,
        grid=(n // tm, m // tn),
        in_specs=[pl.BlockSpec((tm, d), lambda i, j: (i, 0)),
                  pl.BlockSpec((None, d, tn), lambda i, j: (layer, 0, j)),
                  pl.BlockSpec((None, d, tn), lambda i, j: (layer, 0, j))],
        out_specs=pl.BlockSpec((tm, tn), lambda i, j: (i, j)),
        out_shape=jax.ShapeDtypeStruct((n, m), BF16),
        compiler_params=_cparams(("parallel", "arbitrary")),
        name="glu",
    )(h, w1, w3)


def _out_norm_kernel(a_ref, w_ref, r_ref, g_ref, x_ref, h_ref):
    half = a_ref.shape[0] // 2
    for r0 in (0, half):
        rs = slice(r0, r0 + half)
        x = r_ref[rs, :] + jnp.dot(a_ref[rs, :], w_ref[...], preferred_element_type=F32)
        x_ref[rs, :] = x
        ms = jnp.mean(x * x, axis=-1, keepdims=True)
        h_ref[rs, :] = (x * lax.rsqrt(ms + RMS_EPS) * g_ref[...]).astype(h_ref.dtype)


def _out_proj_norm(a, w, res, g, tm):
    n, k = a.shape
    d = w.shape[1]
    return pl.pallas_call(
        _out_norm_kernel,
        grid=(n // tm,),
        in_specs=[pl.BlockSpec((tm, k), lambda i: (i, 0)),
                  pl.BlockSpec((k, d), lambda i: (0, 0)),
                  pl.BlockSpec((tm, d), lambda i: (i, 0)),
                  pl.BlockSpec((1, d), lambda i: (0, 0))],
        out_specs=[pl.BlockSpec((tm, d), lambda i: (i, 0)),
                   pl.BlockSpec((tm, d), lambda i: (i, 0))],
        out_shape=[jax.ShapeDtypeStruct((n, d), F32), jax.ShapeDtypeStruct((n, d), BF16)],
        compiler_params=_cparams(("parallel",)),
        name="out_proj_norm",
    )(a, w, res, g.reshape(1, d))


def _mm_res_kernel(a_ref, w_ref, r_ref, o_ref):
    for c0 in range(0, o_ref.shape[1], MXU_N):
        cs = slice(c0, c0 + MXU_N)
        o_ref[:, cs] = r_ref[:, cs] + jnp.dot(a_ref[...], w_ref[:, cs], preferred_element_type=F32)


def _matmul_residual(a, w, layer, res, tm, tn):
    n, k = a.shape
    m = w.shape[2]
    return pl.pallas_call(
        _mm_res_kernel,
        grid=(n // tm, m // tn),
        in_specs=[pl.BlockSpec((tm, k), lambda i, j: (i, 0)),
                  pl.BlockSpec((None, k, tn), lambda i, j: (layer, 0, j)),
                  pl.BlockSpec((tm, tn), lambda i, j: (i, j))],
        out_specs=pl.BlockSpec((tm, tn), lambda i, j: (i, j)),
        out_shape=jax.ShapeDtypeStruct((n, m), F32),
        compiler_params=_cparams(("parallel", "arbitrary")),
        name="matmul_residual",
    )(a, w, res)


def _final_norm_kernel(x_ref, g_ref, o_ref):
    _rmsnorm_rows(o_ref, x_ref, g_ref, 128)


def _final_norm(x, g, tm):
    n, d = x.shape
    return pl.pallas_call(
        _final_norm_kernel,
        grid=(n // tm,),
        in_specs=[pl.BlockSpec((tm, d), lambda i: (i, 0)),
                  pl.BlockSpec((1, d), lambda i: (0, 0))],
        out_specs=pl.BlockSpec((tm, d), lambda i: (i, 0)),
        out_shape=jax.ShapeDtypeStruct((n, d), F32),
        compiler_params=_cparams(("parallel",)),
        name="final_norm",
    )(x, g.reshape(1, d))


def _split3_bf16(x):
    hi = x.astype(BF16)
    r1 = x - hi.astype(F32)
    mid = r1.astype(BF16)
    lo = (r1 - mid.astype(F32)).astype(BF16)
    return hi, mid, lo


def _gla_kernel(q_ref, k_ref, v_ref, g_ref, a_ref, wa2_ref, ba_ref, gn_ref, o_ref, st_ref,
                *, heads, dk, dv):
    tb = q_ref.shape[0]

    @pl.when(pl.program_id(1) == 0)
    def _():
        st_ref[...] = jnp.zeros_like(st_ref)

    z = jnp.dot(a_ref[...].astype(BF16), wa2_ref[...], preferred_element_type=F32) + ba_ref[...]
    la = (jnp.minimum(z, 0.0) - jnp.log1p(jnp.exp(-jnp.abs(z)))) * (1.0 / GLA_GATE_TAU)

    ri = lax.broadcasted_iota(jnp.int32, (CHUNK, CHUNK), 0)
    ci = lax.broadcasted_iota(jnp.int32, (CHUNK, CHUNK), 1)
    tri = jnp.where(ri >= ci, 1.0, 0.0).astype(BF16)
    q_scale = dk ** -0.5

    for c in range(tb // CHUNK):
        rs = slice(c * CHUNK, (c + 1) * CHUNK)
        hi, mid, lo = _split3_bf16(la[rs, :])
        lcum = (jnp.dot(tri, hi, preferred_element_type=F32)
                + jnp.dot(tri, mid, preferred_element_type=F32)
                + jnp.dot(tri, lo, preferred_element_type=F32))
        ltot = lcum[CHUNK - 1:CHUNK, :]
        kd = (k_ref[rs, :] * jnp.exp(ltot - lcum)).astype(BF16)
        dec = jnp.exp(ltot)
        qc = (q_ref[rs, :] * q_scale).astype(BF16)
        for h in range(heads):
            ks = slice(h * dk, (h + 1) * dk)
            vs = slice(h * dv, (h + 1) * dv)
            vt = v_ref[rs, vs].T.astype(BF16)
            s_new = st_ref[h] * dec[:, ks] + jnp.dot(vt, kd[:, ks], preferred_element_type=F32)
            st_ref[h] = s_new
            oh = lax.dot_general(qc[:, ks], s_new.astype(BF16), _NT,
                                 preferred_element_type=F32)
            ms = jnp.mean(oh * oh, axis=-1, keepdims=True)
            y = oh * lax.rsqrt(ms + RMS_EPS) * gn_ref[...]
            gate = g_ref[rs, vs].astype(F32)
            y = y * (gate * (1.0 / (1.0 + jnp.exp(-gate))))
            o_ref[rs, vs] = y.astype(o_ref.dtype)


def _gla_scan(proj, wa2p, ba, gn, batch, seq, d_qk, d_v, tb):
    n = proj.shape[0]
    nt = seq // tb
    heads = GLA_HEADS
    dk, dv = d_qk // heads, d_v // heads
    row = lambda b, t: b * nt + t
    kern = functools.partial(_gla_kernel, heads=heads, dk=dk, dv=dv)
    return pl.pallas_call(
        kern,
        grid=(batch, nt),
        in_specs=[pl.BlockSpec((tb, d_qk), lambda b, t: (row(b, t), 0)),
                  pl.BlockSpec((tb, d_qk), lambda b, t: (row(b, t), 1)),
                  pl.BlockSpec((tb, d_v), lambda b, t: (row(b, t), (2 * d_qk) // d_v)),
                  pl.BlockSpec((tb, d_v), lambda b, t: (row(b, t), (2 * d_qk + d_v) // d_v)),
                  pl.BlockSpec((tb, LANE), lambda b, t: (row(b, t), (2 * d_qk + 2 * d_v) // LANE)),
                  pl.BlockSpec((LANE, d_qk), lambda b, t: (0, 0)),
                  pl.BlockSpec((1, d_qk), lambda b, t: (0, 0)),
                  pl.BlockSpec((1, dv), lambda b, t: (0, 0))],
        out_specs=pl.BlockSpec((tb, d_v), lambda b, t: (row(b, t), 0)),
        out_shape=jax.ShapeDtypeStruct((n, d_v), BF16),
        scratch_shapes=[pltpu.VMEM((heads, dv, dk), F32)],
        compiler_params=_cparams(("arbitrary", "arbitrary")),
        name="gla_scan",
    )(proj, proj, proj, proj, proj, wa2p, ba.reshape(1, d_qk), gn.reshape(1, dv))


def _t5_bucket(rel):
    nb = REL_BUCKETS // 2
    max_exact = nb // 2
    ret = (rel > 0).astype(jnp.int32) * nb
    n = jnp.abs(rel)
    large = max_exact + (jnp.log(jnp.maximum(n, 1).astype(jnp.float32) / max_exact)
                         / math.log(REL_MAX_DIST / max_exact) * (nb - max_exact)).astype(jnp.int32)
    large = jnp.minimum(large, nb - 1)
    return ret + jnp.where(n < max_exact, n, large)


def _fold_rows(x, op):
    parts = [x[r * SUBLANE:(r + 1) * SUBLANE, :] for r in range(x.shape[0] // SUBLANE)]
    while len(parts) > 1:
        parts = [op(parts[i], parts[i + 1]) for i in range(0, len(parts), 2)]
    return parts[0]


def _dsa_kernel(*refs, topk):
    npair = IDX_HEADS // 2
    q_ref, c_ref, ki_ref, wi_ref = refs[:4]
    qip_refs = refs[4:4 + npair]
    (kvn_ref, kin_ref, wuvt_ref, bktt_ref, rb_ref, o_ref,
     crow_ref, ccol_ref, kirow_ref, bias_ref, scf_ref, scn_ref, lo_ref, lg0_ref, pp_ref, pa_ref,
     m_ref, l_ref, acc_ref) = refs[4 + npair:]
    b = pl.program_id(0)
    j = pl.program_id(1)
    H, QB, LAT = DSA_HEADS, Q_BLOCK, DSA_LATENT
    PAIR = 2 * QB
    far_bucket = REL_BUCKETS // 2 - 1
    n_far_max = scf_ref.shape[0]

    @pl.when((b == 0) & (j == 0))
    def _init():
        crow_ref[...] = jnp.zeros_like(crow_ref)
        kirow_ref[...] = jnp.zeros_like(kirow_ref)
        ccol_ref[:, 0:LAT, :] = jnp.zeros((ccol_ref.shape[0], LAT, QB), BF16)
        ccol_ref[:, LAT:, :] = jnp.ones((ccol_ref.shape[0], ONES_ROWS, QB), BF16)
        bk = bktt_ref[...]

        def head_bias(h, c):
            far = rb_ref[far_bucket, h]
            t = jnp.zeros(bk.shape, F32)
            for bb in range(REL_BUCKETS):
                t = jnp.where(bk == bb, (rb_ref[bb, h] - far) * LOG2E, t)
            bias_ref[h] = t
            return c
        lax.fori_loop(0, H, head_bias, 0)

    r_own = pl.multiple_of((j + 1) * QB, QB)
    c = c_ref[...].astype(F32)
    cn = c * lax.rsqrt(jnp.mean(c * c, axis=-1, keepdims=True) + RMS_EPS) * kvn_ref[...]
    crow_ref[pl.ds(r_own, QB), :] = (cn * (LAT ** -0.5 * LOG2E)).astype(BF16)
    ccol_ref[j + 1, 0:LAT, :] = cn.T.astype(BF16)
    ki = ki_ref[...].astype(F32)
    kin = ki * lax.rsqrt(jnp.mean(ki * ki, axis=-1, keepdims=True) + RMS_EPS) * kin_ref[...]
    kirow_ref[pl.ds(r_own, QB), :] = kin.astype(BF16)

    def q_group(p):
        return jnp.concatenate([q_ref[:, (ATT_GROUP * p + u) * LAT:(ATT_GROUP * p + u + 1) * LAT]
                                for u in range(ATT_GROUP)], axis=0)

    def qi_pair(p):
        return jnp.concatenate([qip_refs[p][:, u * IDX_DIM:(u + 1) * IDX_DIM] for u in range(2)], axis=0)

    wit = (wi_ref[...].astype(F32) * (IDX_HEADS ** -0.5 * IDX_DIM ** -0.5)).T

    nf = lax.shift_right_logical(j + 2, 2)
    tq_chunk = lax.shift_right_logical(lax.broadcasted_iota(jnp.int32, (1, QB), 1), int(math.log2(CHUNK)))
    rowf = lax.broadcasted_iota(jnp.int32, (FAR_W, QB), 0)
    rown = lax.broadcasted_iota(jnp.int32, (NEAR_W, QB), 0)
    adm_n = (rown < CHUNK * (QB // CHUNK + 1 + tq_chunk)) & ((j >= 1) | (rown >= QB))

    def score_of(kib):
        s = None
        sc_next = lax.dot_general(kib, qi_pair(0), _NT, preferred_element_type=F32)
        for p in range(H // 2):
            sc = sc_next
            if p + 1 < H // 2:
                sc_next = lax.dot_general(kib, qi_pair(p + 1), _NT, preferred_element_type=F32)
            for u in range(2):
                h = 2 * p + u
                t = jnp.maximum(sc[:, u * QB:(u + 1) * QB], 0.0) * wit[h:h + 1, :]
                s = t if s is None else s + t
        return s

    def far_rows(mb):
        return pl.ds(pl.multiple_of(QB + mb * FAR_W, QB), FAR_W)

    def far_score(mb, carry):
        mn, mx = carry
        s = score_of(kirow_ref[far_rows(mb), :])
        adm = rowf < (j - 1) * QB - mb * FAR_W
        scf_ref[mb] = jnp.where(adm, s, -jnp.inf)
        mx = jnp.maximum(mx, _fold_rows(jnp.where(adm, s, -jnp.inf), jnp.maximum))
        mn = jnp.minimum(mn, _fold_rows(jnp.where(adm, s, jnp.inf), jnp.minimum))
        return mn, mx

    mn, mx = lax.fori_loop(0, nf, far_score,
                           (jnp.full((SUBLANE, QB), jnp.inf, F32), jnp.full((SUBLANE, QB), -jnp.inf, F32)))
    r_near = pl.multiple_of(j * QB, QB)
    s = score_of(kirow_ref[pl.ds(r_near, NEAR_W), :])
    scn_ref[...] = jnp.where(adm_n, s, -jnp.inf)
    mx = jnp.maximum(mx, _fold_rows(jnp.where(adm_n, s, -jnp.inf), jnp.maximum))
    mn = jnp.minimum(mn, _fold_rows(jnp.where(adm_n, s, jnp.inf), jnp.minimum))
    mx = jnp.max(mx, axis=0, keepdims=True)
    mn = jnp.min(mn, axis=0, keepdims=True)

    kf = float(topk)

    def fold_scores(fn, op, init):
        def fb(mb, acc):
            return op(acc, _fold_rows(fn(scf_ref[mb]), op))
        acc = lax.fori_loop(0, nf, fb, jnp.full((SUBLANE, QB), init, F32))
        return op(acc, _fold_rows(fn(scn_ref[...]), op))

    def count_ge(t):
        return jnp.sum(fold_scores(lambda v: jnp.where(v >= t, 1.0, 0.0), jnp.add, 0.0),
                       axis=0, keepdims=True)

    def max_below(t):
        return jnp.max(fold_scores(lambda v: jnp.where(v < t, v, -jnp.inf), jnp.maximum, -jnp.inf),
                       axis=0, keepdims=True)

    def bis_step(lo, hi, clo, chi):
        mid = 0.5 * lo + 0.5 * hi
        cm = count_ge(mid)
        take = cm >= kf
        stuck = (mid <= lo) | (mid >= hi)
        lo2, clo2 = jnp.where(take, mid, lo), jnp.where(take, cm, clo)
        hi2, chi2 = jnp.where(take, hi, mid), jnp.where(take, chi, cm)
        narrow = (clo2 <= kf) | (clo2 - chi2 <= SNAP_POP) | stuck
        return lo2, hi2, clo2, chi2, narrow, stuck

    def bis_cond(carry):
        return (carry[-1] > 0.0) & (carry[0] < MAX_BISECT)

    def bis_body(carry):
        it, lo, hi, clo, chi, _, _ = carry
        narrow = stuck = None
        for _ in range(BISECT_UNROLL):
            lo, hi, clo, chi, narrow, stuck = bis_step(lo, hi, clo, chi)
        return (it + BISECT_UNROLL, lo, hi, clo, chi, jnp.where(stuck, 1.0, 0.0),
                jnp.max(jnp.where(narrow, 0.0, 1.0)))

    n_adm = ((2 * j + tq_chunk + 1) * CHUNK).astype(F32)
    hi0 = mx + (jnp.abs(mx) * 2.0 ** -20 + 1e-30)
    zero = jnp.zeros((1, QB), F32)
    wide0 = jnp.max(jnp.where(n_adm <= kf, 0.0, 1.0))
    _, lo, hi, clo, chi, stuck, _ = lax.while_loop(
        bis_cond, bis_body, (jnp.int32(0), mn, hi0, n_adm, zero, zero, wide0))

    def snap_active(clo, stuck):
        return (clo > kf) & (stuck == 0.0)

    def snap_cond(carry):
        return (carry[-1] > 0.0) & (carry[0] < SNAP_POP)

    def snap_body(carry):
        it, lo, hi, clo, chi, _ = carry
        t = max_below(hi)
        c = chi + 1.0
        active = snap_active(clo, stuck)
        fin = active & (c >= kf)
        step = active & jnp.logical_not(fin)
        lo, clo = jnp.where(fin, t, lo), jnp.where(fin, c, clo)
        hi, chi = jnp.where(step, t, hi), jnp.where(step, c, chi)
        return it + 1, lo, hi, clo, chi, jnp.max(jnp.where(snap_active(clo, stuck), 1.0, 0.0))

    snap0 = jnp.max(jnp.where(snap_active(clo, stuck), 1.0, 0.0))
    _, lo, _, _, _, _ = lax.while_loop(snap_cond, snap_body, (jnp.int32(0), lo, hi, clo, chi, snap0))

    over = count_ge(lo) > kf
    lo_ref[...] = lo

    @pl.when(jnp.max(jnp.where(over, 1.0, 0.0)) > 0.0)
    def _exact_ties():
        def min_above(t):
            return jnp.min(fold_scores(lambda v: jnp.where(v > t, v, jnp.inf), jnp.minimum, jnp.inf),
                           axis=0, keepdims=True)

        def up_cond(carry):
            return (carry[-1] > 0.0) & (carry[0] < MAX_BISECT)

        def up_body(carry):
            it, cur, _ = carry
            nxt = min_above(cur)
            move = over & (count_ge(nxt) >= kf) & (nxt < jnp.inf)
            return it + 1, jnp.where(move, nxt, cur), jnp.max(jnp.where(move, 1.0, 0.0))

        _, thr, _ = lax.while_loop(up_cond, up_body, (jnp.int32(0), lo, jnp.float32(1.0)))
        lo_ref[...] = thr
        above = jnp.sum(fold_scores(lambda v: jnp.where(v > thr, 1.0, 0.0), jnp.add, 0.0),
                        axis=0, keepdims=True)
        keep = kf - above
        rr = lax.broadcasted_iota(jnp.int32, (FAR_W, FAR_W), 0)
        cc = lax.broadcasted_iota(jnp.int32, (FAR_W, FAR_W), 1)
        tri = jnp.where(rr >= cc, 1.0, 0.0).astype(BF16)

        def strike(v, seen, w):
            tie = (v == thr) & over
            rank = seen + jnp.dot(tri[0:w, 0:w], jnp.where(tie, 1.0, 0.0).astype(BF16),
                                  preferred_element_type=F32)
            return jnp.where(tie & (rank > keep), -jnp.inf, v), rank[w - 1:w, :]

        def far_strike(mb, seen):
            v, seen = strike(scf_ref[mb], seen, FAR_W)
            scf_ref[mb] = v
            return seen

        seen = lax.fori_loop(0, nf, far_strike, jnp.zeros((1, QB), F32))
        scn_ref[...] = strike(scn_ref[...], seen, NEAR_W)[0]

    lo = lo_ref[...]

    m_ref[...] = jnp.full(m_ref.shape, MASK_NEG, F32)
    l_ref[...] = jnp.zeros_like(l_ref)
    acc_ref[...] = jnp.zeros_like(acc_ref)

    def logits(crow, p):
        return lax.dot_general(crow, q_group(p), _NT, preferred_element_type=F32)

    gw = ATT_GROUP * QB
    ngroup = H // ATT_GROUP
    last_ps = slice((ngroup - 1) * gw, ngroup * gw)

    def accumulate(ps, prob, alpha, ccol):
        pv = jnp.dot(ccol, prob, preferred_element_type=F32)
        acc_ref[:, ps] = acc_ref[:, ps] * alpha + pv[0:LAT, :]
        l_ref[:, ps] = l_ref[:, ps] * alpha + pv[LAT:LAT + 1, :]

    def attend(crow, ccol, madd, near, lg_first, pending):
        lg_next = lg_first if lg_first is not None else logits(crow, 0)
        for p in range(ngroup):
            ps = slice(p * gw, (p + 1) * gw)
            lg = lg_next
            if p + 1 < ngroup:
                lg_next = logits(crow, p + 1)
            probs, alphas = [], []
            for u in range(ATT_GROUP):
                h = ATT_GROUP * p + u
                hs = slice(h * QB, (h + 1) * QB)
                sh = lg[:, u * QB:(u + 1) * QB] + madd
                if near:
                    sh = sh + bias_ref[h]
                m_old = m_ref[:, hs]
                m_new = jnp.maximum(m_old, jnp.max(sh, axis=0, keepdims=True))
                m_ref[:, hs] = m_new
                probs.append(jnp.exp2(sh - m_new).astype(BF16))
                alphas.append(jnp.exp2(m_old - m_new))
            accumulate(*pending)
            pending = (ps, jnp.concatenate(probs, axis=1), jnp.concatenate(alphas, axis=1), ccol)
        return pending

    def far_ccol(mb):
        blk = 1 + mb * (FAR_W // QB)
        return jnp.concatenate([ccol_ref[blk + i] for i in range(FAR_W // QB)], axis=1)

    lg0_ref[...] = logits(crow_ref[far_rows(0), :], 0)
    pp_ref[...] = jnp.zeros_like(pp_ref)
    pa_ref[...] = jnp.ones_like(pa_ref)

    def far_attend(mb, carry):
        madd = jnp.where(scf_ref[mb] >= lo, 0.0, MASK_NEG)
        carried = (last_ps, pp_ref[...], pa_ref[...], far_ccol(jnp.maximum(mb - 1, 0)))
        _, prob, alpha, _ = attend(crow_ref[far_rows(mb), :], far_ccol(mb), madd, False, lg0_ref[...], carried)
        pp_ref[...] = prob
        pa_ref[...] = alpha
        lg0_ref[...] = logits(crow_ref[far_rows(jnp.minimum(mb + 1, n_far_max - 1)), :], 0)
        return carry

    lax.fori_loop(0, nf, far_attend, 0)
    madd = jnp.where(scn_ref[...] >= lo, 0.0, MASK_NEG)
    carried = (last_ps, pp_ref[...], pa_ref[...], far_ccol(jnp.maximum(nf - 1, 0)))
    ccol = jnp.concatenate([ccol_ref[j], ccol_ref[j + 1]], axis=1)
    accumulate(*attend(crow_ref[pl.ds(r_near, NEAR_W), :], ccol, madd, True, None, carried))

    dvh = wuvt_ref.shape[1]
    for h in range(H):
        hs = slice(h * QB, (h + 1) * QB)
        oht = (acc_ref[:, hs] * (1.0 / l_ref[:, hs])).astype(BF16)
        out_t = jnp.dot(wuvt_ref[h], oht, preferred_element_type=F32)
        o_ref[:, h * dvh:(h + 1) * dvh] = out_t.T.astype(o_ref.dtype)


def _dsa_attention(proj, kvn, kin, wuvt, bktt, rel_bias, batch, seq, d_out):
    n = proj.shape[0]
    nq = seq // Q_BLOCK
    H, QB, LAT = DSA_HEADS, Q_BLOCK, DSA_LATENT
    topk = min(TOPK_MAX, seq // 4)
    row = lambda b, j: b * nq + j
    o_c = H * LAT
    o_qi = o_c + LAT
    o_ki = o_qi + IDX_HEADS * IDX_DIM
    o_wi = o_ki + IDX_DIM
    pair_w = 2 * IDX_DIM
    n_far = max(1, (nq + 1) // 4)
    const = lambda b, j: (0, 0)
    return pl.pallas_call(
        functools.partial(_dsa_kernel, topk=topk),
        grid=(batch, nq),
        in_specs=[pl.BlockSpec((QB, H * LAT), lambda b, j: (row(b, j), 0)),
                  pl.BlockSpec((QB, LAT), lambda b, j: (row(b, j), o_c // LAT)),
                  pl.BlockSpec((QB, IDX_DIM), lambda b, j: (row(b, j), o_ki // IDX_DIM)),
                  pl.BlockSpec((QB, LANE), lambda b, j: (row(b, j), o_wi // LANE))]
                 + [pl.BlockSpec((QB, pair_w), functools.partial(
                     lambda b, j, p: (row(b, j), o_qi // pair_w + p), p=p)) for p in range(IDX_HEADS // 2)]
                 + [pl.BlockSpec((1, LAT), const),
                    pl.BlockSpec((1, IDX_DIM), const),
                    pl.BlockSpec(wuvt.shape, lambda b, j: (0, 0, 0)),
                    pl.BlockSpec((NEAR_W, QB), const),
                    pl.BlockSpec(memory_space=pltpu.SMEM)],
        out_specs=pl.BlockSpec((QB, d_out), lambda b, j: (row(b, j), 0)),
        out_shape=jax.ShapeDtypeStruct((n, d_out), BF16),
        scratch_shapes=[pltpu.VMEM((seq + QB, LAT), BF16),
                        pltpu.VMEM((nq + 1, LAT + ONES_ROWS, QB), BF16),
                        pltpu.VMEM((seq + QB, IDX_DIM), BF16),
                        pltpu.VMEM((H, NEAR_W, QB), F32),
                        pltpu.VMEM((n_far, FAR_W, QB), F32),
                        pltpu.VMEM((NEAR_W, QB), F32),
                        pltpu.VMEM((1, QB), F32),
                        pltpu.VMEM((FAR_W, ATT_GROUP * QB), F32),
                        pltpu.VMEM((FAR_W, ATT_GROUP * QB), BF16),
                        pltpu.VMEM((1, ATT_GROUP * QB), F32),
                        pltpu.VMEM((1, H * QB), F32),
                        pltpu.VMEM((1, H * QB), F32),
                        pltpu.VMEM((LAT, H * QB), F32)],
        compiler_params=_cparams(("arbitrary", "arbitrary")),
        name="dsa_attention",
    )(*([proj] * (4 + IDX_HEADS // 2)), kvn.reshape(1, LAT), kin.reshape(1, IDX_DIM), wuvt, bktt, rel_bias)


def kernel(x, norm_mix, norm_ffn, norm_final, gla_w_in, gla_w_a2, gla_b_a, gla_g_norm, gla_w_out,
           dsa_w_in, dsa_kv_norm, dsa_kidx_norm, dsa_w_uv, dsa_w_out, rel_bias, ffn_w1, ffn_w3, ffn_w2):
    batch, seq, d = x.shape
    depth = norm_mix.shape[0]
    n = batch * seq
    d_qk = gla_w_a2.shape[2]
    d_v = gla_w_out.shape[1]
    tn = 512

    wk = jnp.arange(NEAR_W, dtype=jnp.int32)[:, None]
    tq = jnp.arange(Q_BLOCK, dtype=jnp.int32)[None, :]
    bktt = _t5_bucket(wk - Q_BLOCK - tq)

    gla_win = _cast_pad_cols(gla_w_in, GLA_PROJ_TN)
    dsa_win = _cast_pad_cols(dsa_w_in, DSA_PROJ_TN)
    w2 = _cast_weights(ffn_w2, tn)
    gla_wout = gla_w_out.astype(BF16)
    dsa_wout = dsa_w_out.astype(BF16)
    wa2p = jnp.pad(gla_w_a2.astype(BF16), ((0, 0), (0, LANE - GLA_GATE_RANK), (0, 0)))
    wuvt = jnp.swapaxes(dsa_w_uv, 2, 3).astype(BF16)

    xf = x.reshape(n, d)
    for i in range(depth):
        jm = i // 2
        if i % 2 == 0:
            proj = _norm_matmul(xf, norm_mix[i], gla_win, jm, min(1024, n), GLA_PROJ_TN)
            u = _gla_scan(proj, wa2p[jm], gla_b_a[jm], gla_g_norm[jm], batch, seq, d_qk, d_v, min(256, seq))
            xf, h = _out_proj_norm(u, gla_wout[jm], xf, norm_ffn[i], min(512, n))
        else:
            proj = _norm_matmul(xf, norm_mix[i], dsa_win, jm, min(1024, n), DSA_PROJ_TN)
            u = _dsa_attention(proj, dsa_kv_norm[jm], dsa_kidx_norm[jm], wuvt[jm],
                               bktt, rel_bias, batch, seq, dsa_w_out.shape[1])
            xf, h = _out_proj_norm(u, dsa_wout[jm], xf, norm_ffn[i], min(512, n))
        act = _glu(h, ffn_w1, ffn_w3, i, min(2048, n), tn)
        xf = _matmul_residual(act, w2, i, xf, min(1024, n), tn)
    return _final_norm(xf, norm_final, min(512, n)).reshape(batch, seq, d)
```

```python
import functools
import math

import jax
import jax.numpy as jnp
from jax import lax
from jax.experimental import pallas as pl
from jax.experimental.pallas import tpu as pltpu

F32 = jnp.float32
BF16 = jnp.bfloat16

RMS_EPS = 1e-6
CHUNK = 64

GLA_HEADS = 4
GLA_GATE_RANK = 16
GLA_GATE_TAU = 16.0

DSA_HEADS = 16
DSA_LATENT = 256
IDX_HEADS = 16
IDX_DIM = 128
TOPK_MAX = 256
Q_BLOCK = 128

REL_BUCKETS = 32
REL_MAX_DIST = 128

MXU_N = 256
LANE = 128
SUBLANE = 8
FAR_W = 512
NEAR_W = 2 * Q_BLOCK
MASK_NEG = -1e30
LOG2E = math.log2(math.e)
ATT_GROUP = 2
ONES_ROWS = 16
BISECT_UNROLL = 7
SNAP_POP = 8
MAX_BISECT = 200
GLA_PROJ_TN = 1280
DSA_PROJ_TN = 1664
VMEM_LIMIT = 56 * 1024 * 1024

_NT = (((1,), (1,)), ((), ()))


def _cparams(sem):
    return pltpu.CompilerParams(dimension_semantics=sem, vmem_limit_bytes=VMEM_LIMIT)


def _rmsnorm_rows(dst_ref, x_ref, g_ref, rows):
    def body(r, c):
        sl = pl.ds(pl.multiple_of(r * rows, rows), rows)
        x = x_ref[sl, :]
        ms = jnp.mean(x * x, axis=-1, keepdims=True)
        dst_ref[sl, :] = (x * lax.rsqrt(ms + RMS_EPS) * g_ref[...]).astype(dst_ref.dtype)
        return c
    lax.fori_loop(0, x_ref.shape[0] // rows, body, 0)


def _dot_cast_w(h_ref, w_ref, cols, kc=256):
    acc = None
    for k0 in range(0, w_ref.shape[0], kc):
        part = jnp.dot(h_ref[:, k0:k0 + kc], w_ref[k0:k0 + kc, cols].astype(BF16),
                       preferred_element_type=F32)
        acc = part if acc is None else acc + part
    return acc


def _cast_kernel(w_ref, o_ref):
    o_ref[...] = w_ref[...].astype(o_ref.dtype)


def _cast_weights(w, tn):
    nl, k, m = w.shape
    return pl.pallas_call(
        _cast_kernel,
        grid=(nl, m // tn),
        in_specs=[pl.BlockSpec((None, k, tn), lambda l, j: (l, 0, j))],
        out_specs=pl.BlockSpec((None, k, tn), lambda l, j: (l, 0, j)),
        out_shape=jax.ShapeDtypeStruct((nl, k, m), BF16),
        compiler_params=_cparams(("parallel", "parallel")),
        name="cast_weights",
    )(w)


def _cast_pad_cols(w, mult):
    return jnp.pad(w.astype(BF16), ((0, 0), (0, 0), (0, (-w.shape[2]) % mult)))


def _norm_mm_kernel(x_ref, g_ref, w_ref, o_ref, h_ref):
    j = pl.program_id(1)

    @pl.when(j == 0)
    def _():
        quarter = x_ref.shape[0] // 4
        for r0 in range(0, x_ref.shape[0], quarter):
            rs = slice(r0, r0 + quarter)
            x = x_ref[rs, :]
            ms = jnp.mean(x * x, axis=-1, keepdims=True)
            h = (x * lax.rsqrt(ms + RMS_EPS) * g_ref[...]).astype(h_ref.dtype)
            h_ref[rs, :] = h
            o_ref[rs, :] = jnp.dot(h, w_ref[...], preferred_element_type=F32).astype(o_ref.dtype)

    @pl.when(j > 0)
    def _():
        o_ref[...] = jnp.dot(h_ref[...], w_ref[...], preferred_element_type=F32).astype(o_ref.dtype)


def _norm_matmul(x, g, w, layer, tm, tn):
    n, d = x.shape
    m = w.shape[2]
    return pl.pallas_call(
        _norm_mm_kernel,
        grid=(n // tm, m // tn),
        in_specs=[pl.BlockSpec((tm, d), lambda i, j: (i, 0)),
                  pl.BlockSpec((1, d), lambda i, j: (0, 0)),
                  pl.BlockSpec((None, d, tn), lambda i, j: (layer, 0, j))],
        out_specs=pl.BlockSpec((tm, tn), lambda i, j: (i, j)),
        out_shape=jax.ShapeDtypeStruct((n, m), BF16),
        scratch_shapes=[pltpu.VMEM((tm, d), BF16)],
        compiler_params=_cparams(("parallel", "arbitrary")),
        name="norm_matmul",
    )(x, g.reshape(1, d), w)


def _glu_kernel(h_ref, w1_ref, w3_ref, o_ref):
    for c0 in range(0, o_ref.shape[1], MXU_N):
        cs = slice(c0, c0 + MXU_N)
        a = _dot_cast_w(h_ref, w1_ref, cs)
        b = _dot_cast_w(h_ref, w3_ref, cs)
        o_ref[:, cs] = (a * (1.0 / (1.0 + jnp.exp(-a))) * b).astype(o_ref.dtype)


def _glu(h, w1, w3, layer, tm, tn):
    n, d = h.shape
    m = w1.shape[2]
    return pl.pallas_call(
        _glu_kernel,
        grid=(n // tm, m // tn),
        in_specs=[pl.BlockSpec((tm, d), lambda i, j: (i, 0)),
                  pl.BlockSpec((None, d, tn), lambda i, j: (layer, 0, j)),
                  pl.BlockSpec((None, d, tn), lambda i, j: (layer, 0, j))],
        out_specs=pl.BlockSpec((tm, tn), lambda i, j: (i, j)),
        out_shape=jax.ShapeDtypeStruct((n, m), BF16),
        compiler_params=_cparams(("parallel", "arbitrary")),
        name="glu",
    )(h, w1, w3)


def _out_norm_kernel(a_ref, w_ref, r_ref, g_ref, x_ref, h_ref):
    half = a_ref.shape[0] // 2
    for r0 in (0, half):
        rs = slice(r0, r0 + half)
        x = r_ref[rs, :] + jnp.dot(a_ref[rs, :], w_ref[...], preferred_element_type=F32)
        x_ref[rs, :] = x
        ms = jnp.mean(x * x, axis=-1, keepdims=True)
        h_ref[rs, :] = (x * lax.rsqrt(ms + RMS_EPS) * g_ref[...]).astype(h_ref.dtype)


def _out_proj_norm(a, w, res, g, tm):
    n, k = a.shape
    d = w.shape[1]
    return pl.pallas_call(
        _out_norm_kernel,
        grid=(n // tm,),
        in_specs=[pl.BlockSpec((tm, k), lambda i: (i, 0)),
                  pl.BlockSpec((k, d), lambda i: (0, 0)),
                  pl.BlockSpec((tm, d), lambda i: (i, 0)),
                  pl.BlockSpec((1, d), lambda i: (0, 0))],
        out_specs=[pl.BlockSpec((tm, d), lambda i: (i, 0)),
                   pl.BlockSpec((tm, d), lambda i: (i, 0))],
        out_shape=[jax.ShapeDtypeStruct((n, d), F32), jax.ShapeDtypeStruct((n, d), BF16)],
        compiler_params=_cparams(("parallel",)),
        name="out_proj_norm",
    )(a, w, res, g.reshape(1, d))


def _mm_res_kernel(a_ref, w_ref, r_ref, o_ref):
    for c0 in range(0, o_ref.shape[1], MXU_N):
        cs = slice(c0, c0 + MXU_N)
        o_ref[:, cs] = r_ref[:, cs] + jnp.dot(a_ref[...], w_ref[:, cs], preferred_element_type=F32)


def _matmul_residual(a, w, layer, res, tm, tn):
    n, k = a.shape
    m = w.shape[2]
    return pl.pallas_call(
        _mm_res_kernel,
        grid=(n // tm, m // tn),
        in_specs=[pl.BlockSpec((tm, k), lambda i, j: (i, 0)),
                  pl.BlockSpec((None, k, tn), lambda i, j: (layer, 0, j)),
                  pl.BlockSpec((tm, tn), lambda i, j: (i, j))],
        out_specs=pl.BlockSpec((tm, tn), lambda i, j: (i, j)),
        out_shape=jax.ShapeDtypeStruct((n, m), F32),
        compiler_params=_cparams(("parallel", "arbitrary")),
        name="matmul_residual",
    )(a, w, res)


def _final_norm_kernel(x_ref, g_ref, o_ref):
    _rmsnorm_rows(o_ref, x_ref, g_ref, 128)


def _final_norm(x, g, tm):
    n, d = x.shape
    return pl.pallas_call(
        _final_norm_kernel,
        grid=(n // tm,),
        in_specs=[pl.BlockSpec((tm, d), lambda i: (i, 0)),
                  pl.BlockSpec((1, d), lambda i: (0, 0))],
        out_specs=pl.BlockSpec((tm, d), lambda i: (i, 0)),
        out_shape=jax.ShapeDtypeStruct((n, d), F32),
        compiler_params=_cparams(("parallel",)),
        name="final_norm",
    )(x, g.reshape(1, d))


def _split3_bf16(x):
    hi = x.astype(BF16)
    r1 = x - hi.astype(F32)
    mid = r1.astype(BF16)
    lo = (r1 - mid.astype(F32)).astype(BF16)
    return hi, mid, lo


def _gla_kernel(q_ref, k_ref, v_ref, g_ref, a_ref, wa2_ref, ba_ref, gn_ref, o_ref, st_ref,
                *, heads, dk, dv):
    tb = q_ref.shape[0]

    @pl.when(pl.program_id(1) == 0)
    def _():
        st_ref[...] = jnp.zeros_like(st_ref)

    z = jnp.dot(a_ref[...].astype(BF16), wa2_ref[...], preferred_element_type=F32) + ba_ref[...]
    la = (jnp.minimum(z, 0.0) - jnp.log1p(jnp.exp(-jnp.abs(z)))) * (1.0 / GLA_GATE_TAU)

    ri = lax.broadcasted_iota(jnp.int32, (CHUNK, CHUNK), 0)
    ci = lax.broadcasted_iota(jnp.int32, (CHUNK, CHUNK), 1)
    tri = jnp.where(ri >= ci, 1.0, 0.0).astype(BF16)
    q_scale = dk ** -0.5

    for c in range(tb // CHUNK):
        rs = slice(c * CHUNK, (c + 1) * CHUNK)
        hi, mid, lo = _split3_bf16(la[rs, :])
        lcum = (jnp.dot(tri, hi, preferred_element_type=F32)
                + jnp.dot(tri, mid, preferred_element_type=F32)
                + jnp.dot(tri, lo, preferred_element_type=F32))
        ltot = lcum[CHUNK - 1:CHUNK, :]
        kd = (k_ref[rs, :] * jnp.exp(ltot - lcum)).astype(BF16)
        dec = jnp.exp(ltot)
        qc = (q_ref[rs, :] * q_scale).astype(BF16)
        for h in range(heads):
            ks = slice(h * dk, (h + 1) * dk)
            vs = slice(h * dv, (h + 1) * dv)
            vt = v_ref[rs, vs].T.astype(BF16)
            s_new = st_ref[h] * dec[:, ks] + jnp.dot(vt, kd[:, ks], preferred_element_type=F32)
            st_ref[h] = s_new
            oh = lax.dot_general(qc[:, ks], s_new.astype(BF16), _NT,
                                 preferred_element_type=F32)
            ms = jnp.mean(oh * oh, axis=-1, keepdims=True)
            y = oh * lax.rsqrt(ms + RMS_EPS) * gn_ref[...]
            gate = g_ref[rs, vs].astype(F32)
            y = y * (gate * (1.0 / (1.0 + jnp.exp(-gate))))
            o_ref[rs, vs] = y.astype(o_ref.dtype)


def _gla_scan(proj, wa2p, ba, gn, batch, seq, d_qk, d_v, tb):
    n = proj.shape[0]
    nt = seq // tb
    heads = GLA_HEADS
    dk, dv = d_qk // heads, d_v // heads
    row = lambda b, t: b * nt + t
    kern = functools.partial(_gla_kernel, heads=heads, dk=dk, dv=dv)
    return pl.pallas_call(
        kern,
        grid=(batch, nt),
        in_specs=[pl.BlockSpec((tb, d_qk), lambda b, t: (row(b, t), 0)),
                  pl.BlockSpec((tb, d_qk), lambda b, t: (row(b, t), 1)),
                  pl.BlockSpec((tb, d_v), lambda b, t: (row(b, t), (2 * d_qk) // d_v)),
                  pl.BlockSpec((tb, d_v), lambda b, t: (row(b, t), (2 * d_qk + d_v) // d_v)),
                  pl.BlockSpec((tb, LANE), lambda b, t: (row(b, t), (2 * d_qk + 2 * d_v) // LANE)),
                  pl.BlockSpec((LANE, d_qk), lambda b, t: (0, 0)),
                  pl.BlockSpec((1, d_qk), lambda b, t: (0, 0)),
                  pl.BlockSpec((1, dv), lambda b, t: (0, 0))],
        out_specs=pl.BlockSpec((tb, d_v), lambda b, t: (row(b, t), 0)),
        out_shape=jax.ShapeDtypeStruct((n, d_v), BF16),
        scratch_shapes=[pltpu.VMEM((heads, dv, dk), F32)],
        compiler_params=_cparams(("arbitrary", "arbitrary")),
        name="gla_scan",
    )(proj, proj, proj, proj, proj, wa2p, ba.reshape(1, d_qk), gn.reshape(1, dv))


def _t5_bucket(rel):
    nb = REL_BUCKETS // 2
    max_exact = nb // 2
    ret = (rel > 0).astype(jnp.int32) * nb
    n = jnp.abs(rel)
    large = max_exact + (jnp.log(jnp.maximum(n, 1).astype(jnp.float32) / max_exact)
                         / math.log(REL_MAX_DIST / max_exact) * (nb - max_exact)).astype(jnp.int32)
    large = jnp.minimum(large, nb - 1)
    return ret + jnp.where(n < max_exact, n, large)


def _fold_rows(x, op):
    parts = [x[r * SUBLANE:(r + 1) * SUBLANE, :] for r in range(x.shape[0] // SUBLANE)]
    while len(parts) > 1:
        parts = [op(parts[i], parts[i + 1]) for i in range(0, len(parts), 2)]
    return parts[0]


def _dsa_kernel(*refs, topk):
    npair = IDX_HEADS // 2
    q_ref, c_ref, ki_ref, wi_ref = refs[:4]
    qip_refs = refs[4:4 + npair]
    (kvn_ref, kin_ref, wuvt_ref, bktt_ref, rb_ref, o_ref,
     crow_ref, ccol_ref, kirow_ref, bias_ref, scf_ref, scn_ref, lo_ref, lg0_ref, pp_ref, pa_ref,
     m_ref, l_ref, acc_ref) = refs[4 + npair:]
    b = pl.program_id(0)
    j = pl.program_id(1)
    H, QB, LAT = DSA_HEADS, Q_BLOCK, DSA_LATENT
    PAIR = 2 * QB
    far_bucket = REL_BUCKETS // 2 - 1
    n_far_max = scf_ref.shape[0]

    @pl.when((b == 0) & (j == 0))
    def _init():
        crow_ref[...] = jnp.zeros_like(crow_ref)
        kirow_ref[...] = jnp.zeros_like(kirow_ref)
        ccol_ref[:, 0:LAT, :] = jnp.zeros((ccol_ref.shape[0], LAT, QB), BF16)
        ccol_ref[:, LAT:, :] = jnp.ones((ccol_ref.shape[0], ONES_ROWS, QB), BF16)
        bk = bktt_ref[...]

        def head_bias(h, c):
            far = rb_ref[far_bucket, h]
            t = jnp.zeros(bk.shape, F32)
            for bb in range(REL_BUCKETS):
                t = jnp.where(bk == bb, (rb_ref[bb, h] - far) * LOG2E, t)
            bias_ref[h] = t
            return c
        lax.fori_loop(0, H, head_bias, 0)

    r_own = pl.multiple_of((j + 1) * QB, QB)
    c = c_ref[...].astype(F32)
    cn = c * lax.rsqrt(jnp.mean(c * c, axis=-1, keepdims=True) + RMS_EPS) * kvn_ref[...]
    crow_ref[pl.ds(r_own, QB), :] = (cn * (LAT ** -0.5 * LOG2E)).astype(BF16)
    ccol_ref[j + 1, 0:LAT, :] = cn.T.astype(BF16)
    ki = ki_ref[...].astype(F32)
    kin = ki * lax.rsqrt(jnp.mean(ki * ki, axis=-1, keepdims=True) + RMS_EPS) * kin_ref[...]
    kirow_ref[pl.ds(r_own, QB), :] = kin.astype(BF16)

    def q_group(p):
        return jnp.concatenate([q_ref[:, (ATT_GROUP * p + u) * LAT:(ATT_GROUP * p + u + 1) * LAT]
                                for u in range(ATT_GROUP)], axis=0)

    def qi_pair(p):
        return jnp.concatenate([qip_refs[p][:, u * IDX_DIM:(u + 1) * IDX_DIM] for u in range(2)], axis=0)

    wit = (wi_ref[...].astype(F32) * (IDX_HEADS ** -0.5 * IDX_DIM ** -0.5)).T

    nf = lax.shift_right_logical(j + 2, 2)
    tq_chunk = lax.shift_right_logical(lax.broadcasted_iota(jnp.int32, (1, QB), 1), int(math.log2(CHUNK)))
    rowf = lax.broadcasted_iota(jnp.int32, (FAR_W, QB), 0)
    rown = lax.broadcasted_iota(jnp.int32, (NEAR_W, QB), 0)
    adm_n = (rown < CHUNK * (QB // CHUNK + 1 + tq_chunk)) & ((j >= 1) | (rown >= QB))

    def score_of(kib):
        s = None
        sc_next = lax.dot_general(kib, qi_pair(0), _NT, preferred_element_type=F32)
        for p in range(H // 2):
            sc = sc_next
            if p + 1 < H // 2:
                sc_next = lax.dot_general(kib, qi_pair(p + 1), _NT, preferred_element_type=F32)
            for u in range(2):
                h = 2 * p + u
                t = jnp.maximum(sc[:, u * QB:(u + 1) * QB], 0.0) * wit[h:h + 1, :]
                s = t if s is None else s + t
        return s

    def far_rows(mb):
        return pl.ds(pl.multiple_of(QB + mb * FAR_W, QB), FAR_W)

    def far_score(mb, carry):
        mn, mx = carry
        s = score_of(kirow_ref[far_rows(mb), :])
        adm = rowf < (j - 1) * QB - mb * FAR_W
        scf_ref[mb] = jnp.where(adm, s, -jnp.inf)
        mx = jnp.maximum(mx, _fold_rows(jnp.where(adm, s, -jnp.inf), jnp.maximum))
        mn = jnp.minimum(mn, _fold_rows(jnp.where(adm, s, jnp.inf), jnp.minimum))
        return mn, mx

    mn, mx = lax.fori_loop(0, nf, far_score,
                           (jnp.full((SUBLANE, QB), jnp.inf, F32), jnp.full((SUBLANE, QB), -jnp.inf, F32)))
    r_near = pl.multiple_of(j * QB, QB)
    s = score_of(kirow_ref[pl.ds(r_near, NEAR_W), :])
    scn_ref[...] = jnp.where(adm_n, s, -jnp.inf)
    mx = jnp.maximum(mx, _fold_rows(jnp.where(adm_n, s, -jnp.inf), jnp.maximum))
    mn = jnp.minimum(mn, _fold_rows(jnp.where(adm_n, s, jnp.inf), jnp.minimum))
    mx = jnp.max(mx, axis=0, keepdims=True)
    mn = jnp.min(mn, axis=0, keepdims=True)

    kf = float(topk)

    def fold_scores(fn, op, init):
        def fb(mb, acc):
            return op(acc, _fold_rows(fn(scf_ref[mb]), op))
        acc = lax.fori_loop(0, nf, fb, jnp.full((SUBLANE, QB), init, F32))
        return op(acc, _fold_rows(fn(scn_ref[...]), op))

    def count_ge(t):
        return jnp.sum(fold_scores(lambda v: jnp.where(v >= t, 1.0, 0.0), jnp.add, 0.0),
                       axis=0, keepdims=True)

    def max_below(t):
        return jnp.max(fold_scores(lambda v: jnp.where(v < t, v, -jnp.inf), jnp.maximum, -jnp.inf),
                       axis=0, keepdims=True)

    def bis_step(lo, hi, clo, chi):
        mid = 0.5 * lo + 0.5 * hi
        cm = count_ge(mid)
        take = cm >= kf
        stuck = (mid <= lo) | (mid >= hi)
        lo2, clo2 = jnp.where(take, mid, lo), jnp.where(take, cm, clo)
        hi2, chi2 = jnp.where(take, hi, mid), jnp.where(take, chi, cm)
        narrow = (clo2 <= kf) | (clo2 - chi2 <= SNAP_POP) | stuck
        return lo2, hi2, clo2, chi2, narrow, stuck

    def bis_cond(carry):
        return (carry[-1] > 0.0) & (carry[0] < MAX_BISECT)

    def bis_body(carry):
        it, lo, hi, clo, chi, _, _ = carry
        narrow = stuck = None
        for _ in range(BISECT_UNROLL):
            lo, hi, clo, chi, narrow, stuck = bis_step(lo, hi, clo, chi)
        return (it + BISECT_UNROLL, lo, hi, clo, chi, jnp.where(stuck, 1.0, 0.0),
                jnp.max(jnp.where(narrow, 0.0, 1.0)))

    n_adm = ((2 * j + tq_chunk + 1) * CHUNK).astype(F32)
    hi0 = mx + (jnp.abs(mx) * 2.0 ** -20 + 1e-30)
    zero = jnp.zeros((1, QB), F32)
    wide0 = jnp.max(jnp.where(n_adm <= kf, 0.0, 1.0))
    _, lo, hi, clo, chi, stuck, _ = lax.while_loop(
        bis_cond, bis_body, (jnp.int32(0), mn, hi0, n_adm, zero, zero, wide0))

    def snap_active(clo, stuck):
        return (clo > kf) & (stuck == 0.0)

    def snap_cond(carry):
        return (carry[-1] > 0.0) & (carry[0] < SNAP_POP)

    def snap_body(carry):
        it, lo, hi, clo, chi, _ = carry
        t = max_below(hi)
        c = chi + 1.0
        active = snap_active(clo, stuck)
        fin = active & (c >= kf)
        step = active & jnp.logical_not(fin)
        lo, clo = jnp.where(fin, t, lo), jnp.where(fin, c, clo)
        hi, chi = jnp.where(step, t, hi), jnp.where(step, c, chi)
        return it + 1, lo, hi, clo, chi, jnp.max(jnp.where(snap_active(clo, stuck), 1.0, 0.0))

    snap0 = jnp.max(jnp.where(snap_active(clo, stuck), 1.0, 0.0))
    _, lo, _, _, _, _ = lax.while_loop(snap_cond, snap_body, (jnp.int32(0), lo, hi, clo, chi, snap0))

    over = count_ge(lo) > kf
    lo_ref[...] = lo

    @pl.when(jnp.max(jnp.where(over, 1.0, 0.0)) > 0.0)
    def _exact_ties():
        def min_above(t):
            return jnp.min(fold_scores(lambda v: jnp.where(v > t, v, jnp.inf), jnp.minimum, jnp.inf),
                           axis=0, keepdims=True)

        def up_cond(carry):
            return (carry[-1] > 0.0) & (carry[0] < MAX_BISECT)

        def up_body(carry):
            it, cur, _ = carry
            nxt = min_above(cur)
            move = over & (count_ge(nxt) >= kf) & (nxt < jnp.inf)
            return it + 1, jnp.where(move, nxt, cur), jnp.max(jnp.where(move, 1.0, 0.0))

        _, thr, _ = lax.while_loop(up_cond, up_body, (jnp.int32(0), lo, jnp.float32(1.0)))
        lo_ref[...] = thr
        above = jnp.sum(fold_scores(lambda v: jnp.where(v > thr, 1.0, 0.0), jnp.add, 0.0),
                        axis=0, keepdims=True)
        keep = kf - above
        rr = lax.broadcasted_iota(jnp.int32, (FAR_W, FAR_W), 0)
        cc = lax.broadcasted_iota(jnp.int32, (FAR_W, FAR_W), 1)
        tri = jnp.where(rr >= cc, 1.0, 0.0).astype(BF16)

        def strike(v, seen, w):
            tie = (v == thr) & over
            rank = seen + jnp.dot(tri[0:w, 0:w], jnp.where(tie, 1.0, 0.0).astype(BF16),
                                  preferred_element_type=F32)
            return jnp.where(tie & (rank > keep), -jnp.inf, v), rank[w - 1:w, :]

        def far_strike(mb, seen):
            v, seen = strike(scf_ref[mb], seen, FAR_W)
            scf_ref[mb] = v
            return seen

        seen = lax.fori_loop(0, nf, far_strike, jnp.zeros((1, QB), F32))
        scn_ref[...] = strike(scn_ref[...], seen, NEAR_W)[0]

    lo = lo_ref[...]

    m_ref[...] = jnp.full(m_ref.shape, MASK_NEG, F32)
    l_ref[...] = jnp.zeros_like(l_ref)
    acc_ref[...] = jnp.zeros_like(acc_ref)

    def logits(crow, p):
        return lax.dot_general(crow, q_group(p), _NT, preferred_element_type=F32)

    gw = ATT_GROUP * QB
    ngroup = H // ATT_GROUP
    last_ps = slice((ngroup - 1) * gw, ngroup * gw)

    def accumulate(ps, prob, alpha, ccol):
        pv = jnp.dot(ccol, prob, preferred_element_type=F32)
        acc_ref[:, ps] = acc_ref[:, ps] * alpha + pv[0:LAT, :]
        l_ref[:, ps] = l_ref[:, ps] * alpha + pv[LAT:LAT + 1, :]

    def attend(crow, ccol, madd, near, lg_first, pending):
        lg_next = lg_first if lg_first is not None else logits(crow, 0)
        for p in range(ngroup):
            ps = slice(p * gw, (p + 1) * gw)
            lg = lg_next
            if p + 1 < ngroup:
                lg_next = logits(crow, p + 1)
            probs, alphas = [], []
            for u in range(ATT_GROUP):
                h = ATT_GROUP * p + u
                hs = slice(h * QB, (h + 1) * QB)
                sh = lg[:, u * QB:(u + 1) * QB] + madd
                if near:
                    sh = sh + bias_ref[h]
                m_old = m_ref[:, hs]
                m_new = jnp.maximum(m_old, jnp.max(sh, axis=0, keepdims=True))
                m_ref[:, hs] = m_new
                probs.append(jnp.exp2(sh - m_new).astype(BF16))
                alphas.append(jnp.exp2(m_old - m_new))
            accumulate(*pending)
            pending = (ps, jnp.concatenate(probs, axis=1), jnp.concatenate(alphas, axis=1), ccol)
        return pending

    def far_ccol(mb):
        blk = 1 + mb * (FAR_W // QB)
        return jnp.concatenate([ccol_ref[blk + i] for i in range(FAR_W // QB)], axis=1)

    lg0_ref[...] = logits(crow_ref[far_rows(0), :], 0)
    pp_ref[...] = jnp.zeros_like(pp_ref)
    pa_ref[...] = jnp.ones_like(pa_ref)

    def far_attend(mb, carry):
        madd = jnp.where(scf_ref[mb] >= lo, 0.0, MASK_NEG)
        carried = (last_ps, pp_ref[...], pa_ref[...], far_ccol(jnp.maximum(mb - 1, 0)))
        _, prob, alpha, _ = attend(crow_ref[far_rows(mb), :], far_ccol(mb), madd, False, lg0_ref[...], carried)
        pp_ref[...] = prob
        pa_ref[...] = alpha
        lg0_ref[...] = logits(crow_ref[far_rows(jnp.minimum(mb + 1, n_far_max - 1)), :], 0)
        return carry

    lax.fori_loop(0, nf, far_attend, 0)
    madd = jnp.where(scn_ref[...] >= lo, 0.0, MASK_NEG)
    carried = (last_ps, pp_ref[...], pa_ref[...], far_ccol(jnp.maximum(nf - 1, 0)))
    ccol = jnp.concatenate([ccol_ref[j], ccol_ref[j + 1]], axis=1)
    accumulate(*attend(crow_ref[pl.ds(r_near, NEAR_W), :], ccol, madd, True, None, carried))

    dvh = wuvt_ref.shape[1]
    for h in range(H):
        hs = slice(h * QB, (h + 1) * QB)
        oht = (acc_ref[:, hs] * (1.0 / l_ref[:, hs])).astype(BF16)
        out_t = jnp.dot(wuvt_ref[h], oht, preferred_element_type=F32)
        o_ref[:, h * dvh:(h + 1) * dvh] = out_t.T.astype(o_ref.dtype)


def _dsa_attention(proj, kvn, kin, wuvt, bktt, rel_bias, batch, seq, d_out):
    n = proj.shape[0]
    nq = seq // Q_BLOCK
    H, QB, LAT = DSA_HEADS, Q_BLOCK, DSA_LATENT
    topk = min(TOPK_MAX, seq // 4)
    row = lambda b, j: b * nq + j
    o_c = H * LAT
    o_qi = o_c + LAT
    o_ki = o_qi + IDX_HEADS * IDX_DIM
    o_wi = o_ki + IDX_DIM
    pair_w = 2 * IDX_DIM
    n_far = max(1, (nq + 1) // 4)
    const = lambda b, j: (0, 0)
    return pl.pallas_call(
        functools.partial(_dsa_kernel, topk=topk),
        grid=(batch, nq),
        in_specs=[pl.BlockSpec((QB, H * LAT), lambda b, j: (row(b, j), 0)),
                  pl.BlockSpec((QB, LAT), lambda b, j: (row(b, j), o_c // LAT)),
                  pl.BlockSpec((QB, IDX_DIM), lambda b, j: (row(b, j), o_ki // IDX_DIM)),
                  pl.BlockSpec((QB, LANE), lambda b, j: (row(b, j), o_wi // LANE))]
                 + [pl.BlockSpec((QB, pair_w), functools.partial(
                     lambda b, j, p: (row(b, j), o_qi // pair_w + p), p=p)) for p in range(IDX_HEADS // 2)]
                 + [pl.BlockSpec((1, LAT), const),
                    pl.BlockSpec((1, IDX_DIM), const),
                    pl.BlockSpec(wuvt.shape, lambda b, j: (0, 0, 0)),
                    pl.BlockSpec((NEAR_W, QB), const),
                    pl.BlockSpec(memory_space=pltpu.SMEM)],
        out_specs=pl.BlockSpec((QB, d_out), lambda b, j: (row(b, j), 0)),
        out_shape=jax.ShapeDtypeStruct((n, d_out), BF16),
        scratch_shapes=[pltpu.VMEM((seq + QB, LAT), BF16),
                        pltpu.VMEM((nq + 1, LAT + ONES_ROWS, QB), BF16),
                        pltpu.VMEM((seq + QB, IDX_DIM), BF16),
                        pltpu.VMEM((H, NEAR_W, QB), F32),
                        pltpu.VMEM((n_far, FAR_W, QB), F32),
                        pltpu.VMEM((NEAR_W, QB), F32),
                        pltpu.VMEM((1, QB), F32),
                        pltpu.VMEM((FAR_W, ATT_GROUP * QB), F32),
                        pltpu.VMEM((FAR_W, ATT_GROUP * QB), BF16),
                        pltpu.VMEM((1, ATT_GROUP * QB), F32),
                        pltpu.VMEM((1, H * QB), F32),
                        pltpu.VMEM((1, H * QB), F32),
                        pltpu.VMEM((LAT, H * QB), F32)],
        compiler_params=_cparams(("arbitrary", "arbitrary")),
        name="dsa_attention",
    )(*([proj] * (4 + IDX_HEADS // 2)), kvn.reshape(1, LAT), kin.reshape(1, IDX_DIM), wuvt, bktt, rel_bias)


def kernel(x, norm_mix, norm_ffn, norm_final, gla_w_in, gla_w_a2, gla_b_a, gla_g_norm, gla_w_out,
           dsa_w_in, dsa_kv_norm, dsa_kidx_norm, dsa_w_uv, dsa_w_out, rel_bias, ffn_w1, ffn_w3, ffn_w2):
    batch, seq, d = x.shape
    depth = norm_mix.shape[0]
    n = batch * seq
    d_qk = gla_w_a2.shape[2]
    d_v = gla_w_out.shape[1]
    tn = 512

    wk = jnp.arange(NEAR_W, dtype=jnp.int32)[:, None]
    tq = jnp.arange(Q_BLOCK, dtype=jnp.int32)[None, :]
    bktt = _t5_bucket(wk - Q_BLOCK - tq)

    gla_win = _cast_pad_cols(gla_w_in, GLA_PROJ_TN)
    dsa_win = _cast_pad_cols(dsa_w_in, DSA_PROJ_TN)
    w2 = _cast_weights(ffn_w2, tn)
    gla_wout = gla_w_out.astype(BF16)
    dsa_wout = dsa_w_out.astype(BF16)
    wa2p = jnp.pad(gla_w_a2.astype(BF16), ((0, 0), (0, LANE - GLA_GATE_RANK), (0, 0)))
    wuvt = jnp.swapaxes(dsa_w_uv, 2, 3).astype(BF16)

    xf = x.reshape(n, d)
    for i in range(depth):
        jm = i // 2
        if i % 2 == 0:
            proj = _norm_matmul(xf, norm_mix[i], gla_win, jm, min(1024, n), GLA_PROJ_TN)
            u = _gla_scan(proj, wa2p[jm], gla_b_a[jm], gla_g_norm[jm], batch, seq, d_qk, d_v, min(256, seq))
            xf, h = _out_proj_norm(u, gla_wout[jm], xf, norm_ffn[i], min(512, n))
        else:
            proj = _norm_matmul(xf, norm_mix[i], dsa_win, jm, min(1024, n), DSA_PROJ_TN)
            u = _dsa_attention(proj, dsa_kv_norm[jm], dsa_kidx_norm[jm], wuvt[jm],
                               bktt, rel_bias, batch, seq, dsa_w_out.shape[1])
            xf, h = _out_proj_norm(u, dsa_wout[jm], xf, norm_ffn[i], min(512, n))
        act = _glu(h, ffn_w1, ffn_w3, i, min(2048, n), tn)
        xf = _matmul_residual(act, w2, i, xf, min(1024, n), tn)
    return _final_norm(xf, norm_final, min(512, n)).reshape(batch, seq, d)
```

```python
import functools
import math

import jax
import jax.numpy as jnp
from jax import lax
from jax.experimental import pallas as pl
from jax.experimental.pallas import tpu as pltpu

F32 = jnp.float32
BF16 = jnp.bfloat16

RMS_EPS = 1e-6
CHUNK = 64

GLA_HEADS = 4
GLA_GATE_RANK = 16
GLA_GATE_TAU = 16.0

DSA_HEADS = 16
DSA_LATENT = 256
IDX_HEADS = 16
IDX_DIM = 128
TOPK_MAX = 256
Q_BLOCK = 128

REL_BUCKETS = 32
REL_MAX_DIST = 128

MXU_N = 256
LANE = 128
SUBLANE = 8
FAR_W = 512
NEAR_W = 2 * Q_BLOCK
MASK_NEG = -1e30
LOG2E = math.log2(math.e)
ATT_GROUP = 2
ONES_ROWS = 16
BISECT_PASSES = 14
MAX_BISECT = 200
GLA_PROJ_TN = 1280
DSA_PROJ_TN = 1664
VMEM_LIMIT = 56 * 1024 * 1024

_NT = (((1,), (1,)), ((), ()))


def _cparams(sem):
    return pltpu.CompilerParams(dimension_semantics=sem, vmem_limit_bytes=VMEM_LIMIT)


def _rmsnorm_rows(dst_ref, x_ref, g_ref, rows):
    def body(r, c):
        sl = pl.ds(pl.multiple_of(r * rows, rows), rows)
        x = x_ref[sl, :]
        ms = jnp.mean(x * x, axis=-1, keepdims=True)
        dst_ref[sl, :] = (x * lax.rsqrt(ms + RMS_EPS) * g_ref[...]).astype(dst_ref.dtype)
        return c
    lax.fori_loop(0, x_ref.shape[0] // rows, body, 0)


def _dot_cast_w(h_ref, w_ref, cols, kc=256):
    acc = None
    for k0 in range(0, w_ref.shape[0], kc):
        part = jnp.dot(h_ref[:, k0:k0 + kc], w_ref[k0:k0 + kc, cols].astype(BF16),
                       preferred_element_type=F32)
        acc = part if acc is None else acc + part
    return acc


def _cast_kernel(w_ref, o_ref):
    o_ref[...] = w_ref[...].astype(o_ref.dtype)


def _cast_weights(w, tn):
    nl, k, m = w.shape
    return pl.pallas_call(
        _cast_kernel,
        grid=(nl, m // tn),
        in_specs=[pl.BlockSpec((None, k, tn), lambda l, j: (l, 0, j))],
        out_specs=pl.BlockSpec((None, k, tn), lambda l, j: (l, 0, j)),
        out_shape=jax.ShapeDtypeStruct((nl, k, m), BF16),
        compiler_params=_cparams(("parallel", "parallel")),
        name="cast_weights",
    )(w)


def _cast_pad_cols(w, mult):
    return jnp.pad(w.astype(BF16), ((0, 0), (0, 0), (0, (-w.shape[2]) % mult)))


def _norm_mm_kernel(x_ref, g_ref, w_ref, o_ref, h_ref):
    j = pl.program_id(1)

    @pl.when(j == 0)
    def _():
        quarter = x_ref.shape[0] // 4
        for r0 in range(0, x_ref.shape[0], quarter):
            rs = slice(r0, r0 + quarter)
            x = x_ref[rs, :]
            ms = jnp.mean(x * x, axis=-1, keepdims=True)
            h = (x * lax.rsqrt(ms + RMS_EPS) * g_ref[...]).astype(h_ref.dtype)
            h_ref[rs, :] = h
            o_ref[rs, :] = jnp.dot(h, w_ref[...], preferred_element_type=F32).astype(o_ref.dtype)

    @pl.when(j > 0)
    def _():
        o_ref[...] = jnp.dot(h_ref[...], w_ref[...], preferred_element_type=F32).astype(o_ref.dtype)


def _norm_matmul(x, g, w, layer, tm, tn):
    n, d = x.shape
    m = w.shape[2]
    return pl.pallas_call(
        _norm_mm_kernel,
        grid=(n // tm, m // tn),
        in_specs=[pl.BlockSpec((tm, d), lambda i, j: (i, 0)),
                  pl.BlockSpec((1, d), lambda i, j: (0, 0)),
                  pl.BlockSpec((None, d, tn), lambda i, j: (layer, 0, j))],
        out_specs=pl.BlockSpec((tm, tn), lambda i, j: (i, j)),
        out_shape=jax.ShapeDtypeStruct((n, m), BF16),
        scratch_shapes=[pltpu.VMEM((tm, d), BF16)],
        compiler_params=_cparams(("parallel", "arbitrary")),
        name="norm_matmul",
    )(x, g.reshape(1, d), w)


def _glu_kernel(h_ref, w1_ref, w3_ref, o_ref):
    for c0 in range(0, o_ref.shape[1], MXU_N):
        cs = slice(c0, c0 + MXU_N)
        a = _dot_cast_w(h_ref, w1_ref, cs)
        b = _dot_cast_w(h_ref, w3_ref, cs)
        o_ref[:, cs] = (a * (1.0 / (1.0 + jnp.exp(-a))) * b).astype(o_ref.dtype)


def _glu(h, w1, w3, layer, tm, tn):
    n, d = h.shape
    m = w1.shape[2]
    return pl.pallas_call(
        _glu_kernel,
        grid=(n // tm, m // tn),
        in_specs=[pl.BlockSpec((tm, d), lambda i, j: (i, 0)),
                  pl.BlockSpec((None, d, tn), lambda i, j: (layer, 0, j)),
                  pl.BlockSpec((None, d, tn), lambda i, j: (layer, 0, j))],
        out_specs=pl.BlockSpec((tm, tn), lambda i, j: (i, j)),
        out_shape=jax.ShapeDtypeStruct((n, m), BF16),
        compiler_params=_cparams(("parallel", "arbitrary")),
        name="glu",
    )(h, w1, w3)


def _out_norm_kernel(a_ref, w_ref, r_ref, g_ref, x_ref, h_ref):
    half = a_ref.shape[0] // 2
    for r0 in (0, half):
        rs = slice(r0, r0 + half)
        x = r_ref[rs, :] + jnp.dot(a_ref[rs, :], w_ref[...], preferred_element_type=F32)
        x_ref[rs, :] = x
        ms = jnp.mean(x * x, axis=-1, keepdims=True)
        h_ref[rs, :] = (x * lax.rsqrt(ms + RMS_EPS) * g_ref[...]).astype(h_ref.dtype)


def _out_proj_norm(a, w, res, g, tm):
    n, k = a.shape
    d = w.shape[1]
    return pl.pallas_call(
        _out_norm_kernel,
        grid=(n // tm,),
        in_specs=[pl.BlockSpec((tm, k), lambda i: (i, 0)),
                  pl.BlockSpec((k, d), lambda i: (0, 0)),
                  pl.BlockSpec((tm, d), lambda i: (i, 0)),
                  pl.BlockSpec((1, d), lambda i: (0, 0))],
        out_specs=[pl.BlockSpec((tm, d), lambda i: (i, 0)),
                   pl.BlockSpec((tm, d), lambda i: (i, 0))],
        out_shape=[jax.ShapeDtypeStruct((n, d), F32), jax.ShapeDtypeStruct((n, d), BF16)],
        compiler_params=_cparams(("parallel",)),
        name="out_proj_norm",
    )(a, w, res, g.reshape(1, d))


def _mm_res_kernel(a_ref, w_ref, r_ref, o_ref):
    for c0 in range(0, o_ref.shape[1], MXU_N):
        cs = slice(c0, c0 + MXU_N)
        o_ref[:, cs] = r_ref[:, cs] + jnp.dot(a_ref[...], w_ref[:, cs], preferred_element_type=F32)


def _matmul_residual(a, w, layer, res, tm, tn):
    n, k = a.shape
    m = w.shape[2]
    return pl.pallas_call(
        _mm_res_kernel,
        grid=(n // tm, m // tn),
        in_specs=[pl.BlockSpec((tm, k), lambda i, j: (i, 0)),
                  pl.BlockSpec((None, k, tn), lambda i, j: (layer, 0, j)),
                  pl.BlockSpec((tm, tn), lambda i, j: (i, j))],
        out_specs=pl.BlockSpec((tm, tn), lambda i, j: (i, j)),
        out_shape=jax.ShapeDtypeStruct((n, m), F32),
        compiler_params=_cparams(("parallel", "arbitrary")),
        name="matmul_residual",
    )(a, w, res)


def _final_norm_kernel(x_ref, g_ref, o_ref):
    _rmsnorm_rows(o_ref, x_ref, g_ref, 128)


def _final_norm(x, g, tm):
    n, d = x.shape
    return pl.pallas_call(
        _final_norm_kernel,
        grid=(n // tm,),
        in_specs=[pl.BlockSpec((tm, d), lambda i: (i, 0)),
                  pl.BlockSpec((1, d), lambda i: (0, 0))],
        out_specs=pl.BlockSpec((tm, d), lambda i: (i, 0)),
        out_shape=jax.ShapeDtypeStruct((n, d), F32),
        compiler_params=_cparams(("parallel",)),
        name="final_norm",
    )(x, g.reshape(1, d))


def _split3_bf16(x):
    hi = x.astype(BF16)
    r1 = x - hi.astype(F32)
    mid = r1.astype(BF16)
    lo = (r1 - mid.astype(F32)).astype(BF16)
    return hi, mid, lo


def _gla_kernel(q_ref, k_ref, v_ref, g_ref, a_ref, wa2_ref, ba_ref, gn_ref, o_ref, st_ref,
                *, heads, dk, dv):
    tb = q_ref.shape[0]

    @pl.when(pl.program_id(1) == 0)
    def _():
        st_ref[...] = jnp.zeros_like(st_ref)

    z = jnp.dot(a_ref[...].astype(BF16), wa2_ref[...], preferred_element_type=F32) + ba_ref[...]
    la = (jnp.minimum(z, 0.0) - jnp.log1p(jnp.exp(-jnp.abs(z)))) * (1.0 / GLA_GATE_TAU)

    ri = lax.broadcasted_iota(jnp.int32, (CHUNK, CHUNK), 0)
    ci = lax.broadcasted_iota(jnp.int32, (CHUNK, CHUNK), 1)
    tri = jnp.where(ri >= ci, 1.0, 0.0).astype(BF16)
    q_scale = dk ** -0.5

    for c in range(tb // CHUNK):
        rs = slice(c * CHUNK, (c + 1) * CHUNK)
        hi, mid, lo = _split3_bf16(la[rs, :])
        lcum = (jnp.dot(tri, hi, preferred_element_type=F32)
                + jnp.dot(tri, mid, preferred_element_type=F32)
                + jnp.dot(tri, lo, preferred_element_type=F32))
        ltot = lcum[CHUNK - 1:CHUNK, :]
        kd = (k_ref[rs, :] * jnp.exp(ltot - lcum)).astype(BF16)
        dec = jnp.exp(ltot)
        qc = (q_ref[rs, :] * q_scale).astype(BF16)
        for h in range(heads):
            ks = slice(h * dk, (h + 1) * dk)
            vs = slice(h * dv, (h + 1) * dv)
            vt = v_ref[rs, vs].T.astype(BF16)
            s_new = st_ref[h] * dec[:, ks] + jnp.dot(vt, kd[:, ks], preferred_element_type=F32)
            st_ref[h] = s_new
            oh = lax.dot_general(qc[:, ks], s_new.astype(BF16), _NT,
                                 preferred_element_type=F32)
            ms = jnp.mean(oh * oh, axis=-1, keepdims=True)
            y = oh * lax.rsqrt(ms + RMS_EPS) * gn_ref[...]
            gate = g_ref[rs, vs].astype(F32)
            y = y * (gate * (1.0 / (1.0 + jnp.exp(-gate))))
            o_ref[rs, vs] = y.astype(o_ref.dtype)


def _gla_scan(proj, wa2p, ba, gn, batch, seq, d_qk, d_v, tb):
    n = proj.shape[0]
    nt = seq // tb
    heads = GLA_HEADS
    dk, dv = d_qk // heads, d_v // heads
    row = lambda b, t: b * nt + t
    kern = functools.partial(_gla_kernel, heads=heads, dk=dk, dv=dv)
    return pl.pallas_call(
        kern,
        grid=(batch, nt),
        in_specs=[pl.BlockSpec((tb, d_qk), lambda b, t: (row(b, t), 0)),
                  pl.BlockSpec((tb, d_qk), lambda b, t: (row(b, t), 1)),
                  pl.BlockSpec((tb, d_v), lambda b, t: (row(b, t), (2 * d_qk) // d_v)),
                  pl.BlockSpec((tb, d_v), lambda b, t: (row(b, t), (2 * d_qk + d_v) // d_v)),
                  pl.BlockSpec((tb, LANE), lambda b, t: (row(b, t), (2 * d_qk + 2 * d_v) // LANE)),
                  pl.BlockSpec((LANE, d_qk), lambda b, t: (0, 0)),
                  pl.BlockSpec((1, d_qk), lambda b, t: (0, 0)),
                  pl.BlockSpec((1, dv), lambda b, t: (0, 0))],
        out_specs=pl.BlockSpec((tb, d_v), lambda b, t: (row(b, t), 0)),
        out_shape=jax.ShapeDtypeStruct((n, d_v), BF16),
        scratch_shapes=[pltpu.VMEM((heads, dv, dk), F32)],
        compiler_params=_cparams(("arbitrary", "arbitrary")),
        name="gla_scan",
    )(proj, proj, proj, proj, proj, wa2p, ba.reshape(1, d_qk), gn.reshape(1, dv))


def _t5_bucket(rel):
    nb = REL_BUCKETS // 2
    max_exact = nb // 2
    ret = (rel > 0).astype(jnp.int32) * nb
    n = jnp.abs(rel)
    large = max_exact + (jnp.log(jnp.maximum(n, 1).astype(jnp.float32) / max_exact)
                         / math.log(REL_MAX_DIST / max_exact) * (nb - max_exact)).astype(jnp.int32)
    large = jnp.minimum(large, nb - 1)
    return ret + jnp.where(n < max_exact, n, large)


def _fold_rows(x, op):
    parts = [x[r * SUBLANE:(r + 1) * SUBLANE, :] for r in range(x.shape[0] // SUBLANE)]
    while len(parts) > 1:
        parts = [op(parts[i], parts[i + 1]) for i in range(0, len(parts), 2)]
    return parts[0]


def _dsa_kernel(*refs, topk):
    npair = IDX_HEADS // 2
    q_ref, c_ref, ki_ref, wi_ref = refs[:4]
    qip_refs = refs[4:4 + npair]
    (kvn_ref, kin_ref, wuvt_ref, bktt_ref, rb_ref, o_ref,
     crow_ref, ccol_ref, kirow_ref, bias_ref, scf_ref, scn_ref, lo_ref, lg0_ref, pp_ref, pa_ref,
     m_ref, l_ref, acc_ref) = refs[4 + npair:]
    b = pl.program_id(0)
    j = pl.program_id(1)
    H, QB, LAT = DSA_HEADS, Q_BLOCK, DSA_LATENT
    PAIR = 2 * QB
    far_bucket = REL_BUCKETS // 2 - 1
    n_far_max = scf_ref.shape[0]

    @pl.when((b == 0) & (j == 0))
    def _init():
        crow_ref[...] = jnp.zeros_like(crow_ref)
        kirow_ref[...] = jnp.zeros_like(kirow_ref)
        ccol_ref[:, 0:LAT, :] = jnp.zeros((ccol_ref.shape[0], LAT, QB), BF16)
        ccol_ref[:, LAT:, :] = jnp.ones((ccol_ref.shape[0], ONES_ROWS, QB), BF16)
        bk = bktt_ref[...]

        def head_bias(h, c):
            far = rb_ref[far_bucket, h]
            t = jnp.zeros(bk.shape, F32)
            for bb in range(REL_BUCKETS):
                t = jnp.where(bk == bb, (rb_ref[bb, h] - far) * LOG2E, t)
            bias_ref[h] = t
            return c
        lax.fori_loop(0, H, head_bias, 0)

    r_own = pl.multiple_of((j + 1) * QB, QB)
    c = c_ref[...].astype(F32)
    cn = c * lax.rsqrt(jnp.mean(c * c, axis=-1, keepdims=True) + RMS_EPS) * kvn_ref[...]
    crow_ref[pl.ds(r_own, QB), :] = (cn * (LAT ** -0.5 * LOG2E)).astype(BF16)
    ccol_ref[j + 1, 0:LAT, :] = cn.T.astype(BF16)
    ki = ki_ref[...].astype(F32)
    kin = ki * lax.rsqrt(jnp.mean(ki * ki, axis=-1, keepdims=True) + RMS_EPS) * kin_ref[...]
    kirow_ref[pl.ds(r_own, QB), :] = kin.astype(BF16)

    def q_group(p):
        return jnp.concatenate([q_ref[:, (ATT_GROUP * p + u) * LAT:(ATT_GROUP * p + u + 1) * LAT]
                                for u in range(ATT_GROUP)], axis=0)

    def qi_pair(p):
        return jnp.concatenate([qip_refs[p][:, u * IDX_DIM:(u + 1) * IDX_DIM] for u in range(2)], axis=0)

    wit = (wi_ref[...].astype(F32) * (IDX_HEADS ** -0.5 * IDX_DIM ** -0.5)).T

    nf = lax.shift_right_logical(j + 2, 2)
    tq_chunk = lax.shift_right_logical(lax.broadcasted_iota(jnp.int32, (1, QB), 1), int(math.log2(CHUNK)))
    rowf = lax.broadcasted_iota(jnp.int32, (FAR_W, QB), 0)
    rown = lax.broadcasted_iota(jnp.int32, (NEAR_W, QB), 0)
    adm_n = (rown < CHUNK * (QB // CHUNK + 1 + tq_chunk)) & ((j >= 1) | (rown >= QB))

    def score_of(kib):
        s = None
        sc_next = lax.dot_general(kib, qi_pair(0), _NT, preferred_element_type=F32)
        for p in range(H // 2):
            sc = sc_next
            if p + 1 < H // 2:
                sc_next = lax.dot_general(kib, qi_pair(p + 1), _NT, preferred_element_type=F32)
            for u in range(2):
                h = 2 * p + u
                t = jnp.maximum(sc[:, u * QB:(u + 1) * QB], 0.0) * wit[h:h + 1, :]
                s = t if s is None else s + t
        return s

    def far_rows(mb):
        return pl.ds(pl.multiple_of(QB + mb * FAR_W, QB), FAR_W)

    def far_score(mb, carry):
        mn, mx = carry
        s = score_of(kirow_ref[far_rows(mb), :])
        adm = rowf < (j - 1) * QB - mb * FAR_W
        scf_ref[mb] = jnp.where(adm, s, -jnp.inf)
        mx = jnp.maximum(mx, _fold_rows(jnp.where(adm, s, -jnp.inf), jnp.maximum))
        mn = jnp.minimum(mn, _fold_rows(jnp.where(adm, s, jnp.inf), jnp.minimum))
        return mn, mx

    mn, mx = lax.fori_loop(0, nf, far_score,
                           (jnp.full((SUBLANE, QB), jnp.inf, F32), jnp.full((SUBLANE, QB), -jnp.inf, F32)))
    r_near = pl.multiple_of(j * QB, QB)
    s = score_of(kirow_ref[pl.ds(r_near, NEAR_W), :])
    scn_ref[...] = jnp.where(adm_n, s, -jnp.inf)
    mx = jnp.maximum(mx, _fold_rows(jnp.where(adm_n, s, -jnp.inf), jnp.maximum))
    mn = jnp.minimum(mn, _fold_rows(jnp.where(adm_n, s, jnp.inf), jnp.minimum))
    mx = jnp.max(mx, axis=0, keepdims=True)
    mn = jnp.min(mn, axis=0, keepdims=True)

    kf = float(topk)

    def fold_scores(fn, op, init):
        def fb(mb, acc):
            return op(acc, _fold_rows(fn(scf_ref[mb]), op))
        acc = lax.fori_loop(0, nf, fb, jnp.full((SUBLANE, QB), init, F32))
        return op(acc, _fold_rows(fn(scn_ref[...]), op))

    def count_ge(t):
        return jnp.sum(fold_scores(lambda v: jnp.where(v >= t, 1.0, 0.0), jnp.add, 0.0),
                       axis=0, keepdims=True)

    def max_below(t):
        return jnp.max(fold_scores(lambda v: jnp.where(v < t, v, -jnp.inf), jnp.maximum, -jnp.inf),
                       axis=0, keepdims=True)

    def bis_step(lo, hi, clo, chi):
        mid = 0.5 * lo + 0.5 * hi
        cm = count_ge(mid)
        take = cm >= kf
        stuck = (mid <= lo) | (mid >= hi)
        lo2, clo2 = jnp.where(take, mid, lo), jnp.where(take, cm, clo)
        hi2, chi2 = jnp.where(take, hi, mid), jnp.where(take, chi, cm)
        return lo2, hi2, clo2, chi2, stuck

    def bis_body(_, carry):
        lo, hi, clo, chi, _ = carry
        lo, hi, clo, chi, stuck = bis_step(lo, hi, clo, chi)
        return lo, hi, clo, chi, jnp.where(stuck, 1.0, 0.0)

    n_adm = ((2 * j + tq_chunk + 1) * CHUNK).astype(F32)
    hi0 = mx + (jnp.abs(mx) * 2.0 ** -20 + 1e-30)
    zero = jnp.zeros((1, QB), F32)
    npass = jnp.where((j * (QB // CHUNK) + QB // CHUNK) * CHUNK > topk, BISECT_PASSES, 0)
    lo, hi, clo, chi, stuck = lax.fori_loop(0, npass, bis_body, (mn, hi0, n_adm, zero, zero))

    def snap_active(clo, stuck):
        return (clo > kf) & (stuck == 0.0)

    def snap_cond(carry):
        return (carry[-1] > 0.0) & (carry[0] < MAX_BISECT)

    def snap_body(carry):
        it, lo, hi, clo, chi, _ = carry
        t = max_below(hi)
        c = chi + 1.0
        active = snap_active(clo, stuck)
        fin = active & (c >= kf)
        step = active & jnp.logical_not(fin)
        lo, clo = jnp.where(fin, t, lo), jnp.where(fin, c, clo)
        hi, chi = jnp.where(step, t, hi), jnp.where(step, c, chi)
        return it + 1, lo, hi, clo, chi, jnp.max(jnp.where(snap_active(clo, stuck), 1.0, 0.0))

    snap0 = jnp.max(jnp.where(snap_active(clo, stuck), 1.0, 0.0))
    _, lo, _, _, _, _ = lax.while_loop(snap_cond, snap_body, (jnp.int32(0), lo, hi, clo, chi, snap0))

    over = count_ge(lo) > kf
    lo_ref[...] = lo

    @pl.when(jnp.max(jnp.where(over, 1.0, 0.0)) > 0.0)
    def _exact_ties():
        def min_above(t):
            return jnp.min(fold_scores(lambda v: jnp.where(v > t, v, jnp.inf), jnp.minimum, jnp.inf),
                           axis=0, keepdims=True)

        def up_cond(carry):
            return (carry[-1] > 0.0) & (carry[0] < MAX_BISECT)

        def up_body(carry):
            it, cur, _ = carry
            nxt = min_above(cur)
            move = over & (count_ge(nxt) >= kf) & (nxt < jnp.inf)
            return it + 1, jnp.where(move, nxt, cur), jnp.max(jnp.where(move, 1.0, 0.0))

        _, thr, _ = lax.while_loop(up_cond, up_body, (jnp.int32(0), lo, jnp.float32(1.0)))
        lo_ref[...] = thr
        above = jnp.sum(fold_scores(lambda v: jnp.where(v > thr, 1.0, 0.0), jnp.add, 0.0),
                        axis=0, keepdims=True)
        keep = kf - above
        rr = lax.broadcasted_iota(jnp.int32, (FAR_W, FAR_W), 0)
        cc = lax.broadcasted_iota(jnp.int32, (FAR_W, FAR_W), 1)
        tri = jnp.where(rr >= cc, 1.0, 0.0).astype(BF16)

        def strike(v, seen, w):
            tie = (v == thr) & over
            rank = seen + jnp.dot(tri[0:w, 0:w], jnp.where(tie, 1.0, 0.0).astype(BF16),
                                  preferred_element_type=F32)
            return jnp.where(tie & (rank > keep), -jnp.inf, v), rank[w - 1:w, :]

        def far_strike(mb, seen):
            v, seen = strike(scf_ref[mb], seen, FAR_W)
            scf_ref[mb] = v
            return seen

        seen = lax.fori_loop(0, nf, far_strike, jnp.zeros((1, QB), F32))
        scn_ref[...] = strike(scn_ref[...], seen, NEAR_W)[0]

    lo = lo_ref[...]

    m_ref[...] = jnp.full(m_ref.shape, MASK_NEG, F32)
    l_ref[...] = jnp.zeros_like(l_ref)
    acc_ref[...] = jnp.zeros_like(acc_ref)

    def logits(crow, p):
        return lax.dot_general(crow, q_group(p), _NT, preferred_element_type=F32)

    gw = ATT_GROUP * QB
    ngroup = H // ATT_GROUP
    last_ps = slice((ngroup - 1) * gw, ngroup * gw)

    def accumulate(ps, prob, alpha, ccol):
        pv = jnp.dot(ccol, prob, preferred_element_type=F32)
        acc_ref[:, ps] = acc_ref[:, ps] * alpha + pv[0:LAT, :]
        l_ref[:, ps] = l_ref[:, ps] * alpha + pv[LAT:LAT + 1, :]

    def attend(crow, ccol, madd, near, lg_first, pending):
        lg_next = lg_first if lg_first is not None else logits(crow, 0)
        for p in range(ngroup):
            ps = slice(p * gw, (p + 1) * gw)
            lg = lg_next
            if p + 1 < ngroup:
                lg_next = logits(crow, p + 1)
            probs, alphas = [], []
            for u in range(ATT_GROUP):
                h = ATT_GROUP * p + u
                hs = slice(h * QB, (h + 1) * QB)
                sh = lg[:, u * QB:(u + 1) * QB] + madd
                if near:
                    sh = sh + bias_ref[h]
                m_old = m_ref[:, hs]
                m_new = jnp.maximum(m_old, jnp.max(sh, axis=0, keepdims=True))
                m_ref[:, hs] = m_new
                probs.append(jnp.exp2(sh - m_new).astype(BF16))
                alphas.append(jnp.exp2(m_old - m_new))
            accumulate(*pending)
            pending = (ps, jnp.concatenate(probs, axis=1), jnp.concatenate(alphas, axis=1), ccol)
        return pending

    def far_ccol(mb):
        blk = 1 + mb * (FAR_W // QB)
        return jnp.concatenate([ccol_ref[blk + i] for i in range(FAR_W // QB)], axis=1)

    lg0_ref[...] = logits(crow_ref[far_rows(0), :], 0)
    pp_ref[...] = jnp.zeros_like(pp_ref)
    pa_ref[...] = jnp.ones_like(pa_ref)

    def far_attend(mb, carry):
        madd = jnp.where(scf_ref[mb] >= lo, 0.0, MASK_NEG)
        carried = (last_ps, pp_ref[...], pa_ref[...], far_ccol(jnp.maximum(mb - 1, 0)))
        _, prob, alpha, _ = attend(crow_ref[far_rows(mb), :], far_ccol(mb), madd, False, lg0_ref[...], carried)
        pp_ref[...] = prob
        pa_ref[...] = alpha
        lg0_ref[...] = logits(crow_ref[far_rows(jnp.minimum(mb + 1, n_far_max - 1)), :], 0)
        return carry

    lax.fori_loop(0, nf, far_attend, 0)
    madd = jnp.where(scn_ref[...] >= lo, 0.0, MASK_NEG)
    carried = (last_ps, pp_ref[...], pa_ref[...], far_ccol(jnp.maximum(nf - 1, 0)))
    ccol = jnp.concatenate([ccol_ref[j], ccol_ref[j + 1]], axis=1)
    accumulate(*attend(crow_ref[pl.ds(r_near, NEAR_W), :], ccol, madd, True, None, carried))

    dvh = wuvt_ref.shape[1]
    for h in range(H):
        hs = slice(h * QB, (h + 1) * QB)
        oht = (acc_ref[:, hs] * (1.0 / l_ref[:, hs])).astype(BF16)
        out_t = jnp.dot(wuvt_ref[h], oht, preferred_element_type=F32)
        o_ref[:, h * dvh:(h + 1) * dvh] = out_t.T.astype(o_ref.dtype)


def _dsa_attention(proj, kvn, kin, wuvt, bktt, rel_bias, batch, seq, d_out):
    n = proj.shape[0]
    nq = seq // Q_BLOCK
    H, QB, LAT = DSA_HEADS, Q_BLOCK, DSA_LATENT
    topk = min(TOPK_MAX, seq // 4)
    row = lambda b, j: b * nq + j
    o_c = H * LAT
    o_qi = o_c + LAT
    o_ki = o_qi + IDX_HEADS * IDX_DIM
    o_wi = o_ki + IDX_DIM
    pair_w = 2 * IDX_DIM
    n_far = max(1, (nq + 1) // 4)
    const = lambda b, j: (0, 0)
    return pl.pallas_call(
        functools.partial(_dsa_kernel, topk=topk),
        grid=(batch, nq),
        in_specs=[pl.BlockSpec((QB, H * LAT), lambda b, j: (row(b, j), 0)),
                  pl.BlockSpec((QB, LAT), lambda b, j: (row(b, j), o_c // LAT)),
                  pl.BlockSpec((QB, IDX_DIM), lambda b, j: (row(b, j), o_ki // IDX_DIM)),
                  pl.BlockSpec((QB, LANE), lambda b, j: (row(b, j), o_wi // LANE))]
                 + [pl.BlockSpec((QB, pair_w), functools.partial(
                     lambda b, j, p: (row(b, j), o_qi // pair_w + p), p=p)) for p in range(IDX_HEADS // 2)]
                 + [pl.BlockSpec((1, LAT), const),
                    pl.BlockSpec((1, IDX_DIM), const),
                    pl.BlockSpec(wuvt.shape, lambda b, j: (0, 0, 0)),
                    pl.BlockSpec((NEAR_W, QB), const),
                    pl.BlockSpec(memory_space=pltpu.SMEM)],
        out_specs=pl.BlockSpec((QB, d_out), lambda b, j: (row(b, j), 0)),
        out_shape=jax.ShapeDtypeStruct((n, d_out), BF16),
        scratch_shapes=[pltpu.VMEM((seq + QB, LAT), BF16),
                        pltpu.VMEM((nq + 1, LAT + ONES_ROWS, QB), BF16),
                        pltpu.VMEM((seq + QB, IDX_DIM), BF16),
                        pltpu.VMEM((H, NEAR_W, QB), F32),
                        pltpu.VMEM((n_far, FAR_W, QB), F32),
                        pltpu.VMEM((NEAR_W, QB), F32),
                        pltpu.VMEM((1, QB), F32),
                        pltpu.VMEM((FAR_W, ATT_GROUP * QB), F32),
                        pltpu.VMEM((FAR_W, ATT_GROUP * QB), BF16),
                        pltpu.VMEM((1, ATT_GROUP * QB), F32),
                        pltpu.VMEM((1, H * QB), F32),
                        pltpu.VMEM((1, H * QB), F32),
                        pltpu.VMEM((LAT, H * QB), F32)],
        compiler_params=_cparams(("arbitrary", "arbitrary")),
        name="dsa_attention",
    )(*([proj] * (4 + IDX_HEADS // 2)), kvn.reshape(1, LAT), kin.reshape(1, IDX_DIM), wuvt, bktt, rel_bias)


def kernel(x, norm_mix, norm_ffn, norm_final, gla_w_in, gla_w_a2, gla_b_a, gla_g_norm, gla_w_out,
           dsa_w_in, dsa_kv_norm, dsa_kidx_norm, dsa_w_uv, dsa_w_out, rel_bias, ffn_w1, ffn_w3, ffn_w2):
    batch, seq, d = x.shape
    depth = norm_mix.shape[0]
    n = batch * seq
    d_qk = gla_w_a2.shape[2]
    d_v = gla_w_out.shape[1]
    tn = 512

    wk = jnp.arange(NEAR_W, dtype=jnp.int32)[:, None]
    tq = jnp.arange(Q_BLOCK, dtype=jnp.int32)[None, :]
    bktt = _t5_bucket(wk - Q_BLOCK - tq)

    gla_win = _cast_pad_cols(gla_w_in, GLA_PROJ_TN)
    dsa_win = _cast_pad_cols(dsa_w_in, DSA_PROJ_TN)
    w2 = _cast_weights(ffn_w2, tn)
    gla_wout = gla_w_out.astype(BF16)
    dsa_wout = dsa_w_out.astype(BF16)
    wa2p = jnp.pad(gla_w_a2.astype(BF16), ((0, 0), (0, LANE - GLA_GATE_RANK), (0, 0)))
    wuvt = jnp.swapaxes(dsa_w_uv, 2, 3).astype(BF16)

    xf = x.reshape(n, d)
    for i in range(depth):
        jm = i // 2
        if i % 2 == 0:
            proj = _norm_matmul(xf, norm_mix[i], gla_win, jm, min(1024, n), GLA_PROJ_TN)
            u = _gla_scan(proj, wa2p[jm], gla_b_a[jm], gla_g_norm[jm], batch, seq, d_qk, d_v, min(256, seq))
            xf, h = _out_proj_norm(u, gla_wout[jm], xf, norm_ffn[i], min(512, n))
        else:
            proj = _norm_matmul(xf, norm_mix[i], dsa_win, jm, min(1024, n), DSA_PROJ_TN)
            u = _dsa_attention(proj, dsa_kv_norm[jm], dsa_kidx_norm[jm], wuvt[jm],
                               bktt, rel_bias, batch, seq, dsa_w_out.shape[1])
            xf, h = _out_proj_norm(u, dsa_wout[jm], xf, norm_ffn[i], min(512, n))
        act = _glu(h, ffn_w1, ffn_w3, i, min(2048, n), tn)
        xf = _matmul_residual(act, w2, i, xf, min(1024, n), tn)
    return _final_norm(xf, norm_final, min(512, n)).reshape(batch, seq, d)
```
